```python
import jax, jax.numpy as jnp
from jax import lax
import numpy as np

D_MODEL = 1024
BATCH = 4
SEQ = 4096
DEPTH = 4
DEC_BATCH = 16
DEC_SEQ = 2048
PAST_LEN = 128

HEAD_DIM = 64
N_HEADS_NA = 6
N_HEADS_WIN = 6
N_KV_WIN = 2
N_HEADS_MEM = 4
D_NA = N_HEADS_NA * HEAD_DIM
D_WIN = N_HEADS_WIN * HEAD_DIM
D_KV_WIN = N_KV_WIN * HEAD_DIM
D_MEM = N_HEADS_MEM * HEAD_DIM
D_MIX = D_NA + D_WIN + D_MEM
D_IN = 4 * D_NA + 2 * D_WIN + 2 * D_KV_WIN + 2 * D_MEM
N_MEM = 256
GRID_W = 64
NA_ROWS_MAX = 8
NA_COLS = 16
NA_QCOLS = 16
NA_KCOLS = 32
WINDOW = 128
WIN_BLOCK = 128
RMS_EPS = 1e-6
NEG_INF = -1e30

kernel_name = "hymba_natten_swa_memory_encoder"


def rms_norm(x, g):
    xf = x.astype(jnp.float32)
    y = xf * lax.rsqrt(jnp.mean(xf * xf, axis=-1, keepdims=True) + RMS_EPS)
    return (y * g.astype(jnp.float32)).astype(x.dtype)


def alibi_slopes(n_heads):
    return 2.0 ** (-8.0 * jnp.arange(1, n_heads + 1, dtype=jnp.float32) / n_heads)


def neighbourhood_attention(q, k, v, rpb):
    b, t, h, d = q.shape
    rows = t // GRID_W
    kh = min(NA_ROWS_MAX, rows)
    qg = q.reshape(b, rows, GRID_W, h, d)
    kg = k.reshape(b, rows, GRID_W, h, d)
    vg = v.reshape(b, rows, GRID_W, h, d)
    r = jnp.arange(rows)
    r0 = jnp.clip(r - kh // 2, 0, rows - kh)
    row_idx = r0[:, None] + jnp.arange(kh)[None, :]
    rel_row = row_idx - r[:, None] + (NA_ROWS_MAX - 1)
    outs = []
    for c0 in range(0, GRID_W, NA_QCOLS):
        kc0 = min(max(c0 - NA_COLS // 2, 0), GRID_W - NA_KCOLS)
        qc = c0 + jnp.arange(NA_QCOLS)
        kc = kc0 + jnp.arange(NA_KCOLS)
        cs = jnp.clip(qc - NA_COLS // 2, 0, GRID_W - NA_COLS)
        col_mask = (kc[None, :] >= cs[:, None]) & (kc[None, :] < cs[:, None] + NA_COLS)
        rel_col = jnp.clip(kc[None, :] - qc[:, None] + NA_COLS - 1, 0, 2 * NA_COLS - 2)
        k_blk = jnp.take(kg[:, :, kc0:kc0 + NA_KCOLS], row_idx, axis=1)
        v_blk = jnp.take(vg[:, :, kc0:kc0 + NA_KCOLS], row_idx, axis=1)
        s = jnp.einsum("brqhd,brkwhd->brhqkw", qg[:, :, c0:c0 + NA_QCOLS], k_blk,
                       preferred_element_type=jnp.float32)
        bias = rpb[:, rel_row[:, None, :, None], rel_col[None, :, None, :]]
        bias = jnp.transpose(bias, (1, 0, 2, 3, 4)).astype(jnp.float32)
        s = s + bias[None] + jnp.where(col_mask, 0.0, NEG_INF)[:, None, :]
        p = jax.nn.softmax(s.reshape(b, rows, h, NA_QCOLS, kh * NA_KCOLS), axis=-1).reshape(s.shape)
        outs.append(jnp.einsum("brhqkw,brkwhd->brqhd", p.astype(v.dtype), v_blk))
    o = jnp.stack(outs, axis=2)
    return o.reshape(b, t, h * d)


def windowed_gqa(q, k, v, sink):
    b, t, h, d = q.shape
    kvh = k.shape[2]
    g = h // kvh
    nb = t // WIN_BLOCK
    qb = q.reshape(b, nb, WIN_BLOCK, kvh, g, d)
    pad = ((0, 0), (WIN_BLOCK, WIN_BLOCK), (0, 0), (0, 0))
    kp = jnp.pad(k, pad).reshape(b, nb + 2, WIN_BLOCK, kvh, d)
    vp = jnp.pad(v, pad).reshape(b, nb + 2, WIN_BLOCK, kvh, d)
    kb = jnp.concatenate([kp[:, :-2], kp[:, 1:-1], kp[:, 2:]], axis=2)
    vb = jnp.concatenate([vp[:, :-2], vp[:, 1:-1], vp[:, 2:]], axis=2)
    s = jnp.einsum("bnqkgd,bnskd->bnkgqs", qb, kb, preferred_element_type=jnp.float32)
    i = jnp.arange(WIN_BLOCK)
    w = jnp.arange(3 * WIN_BLOCK)
    dist = (i[:, None] - w[None, :] + WIN_BLOCK).astype(jnp.float32)
    pos_s = jnp.arange(nb)[:, None] * WIN_BLOCK - WIN_BLOCK + w[None, :]
    valid = (jnp.abs(dist) <= WINDOW)[None] & ((pos_s >= 0) & (pos_s < t))[:, None, :]
    slopes = alibi_slopes(h).reshape(kvh, g)
    s = s - slopes[:, :, None, None] * jnp.abs(dist)[None, None]
    s = jnp.where(valid[None, :, None, None], s, NEG_INF)
    sink_l = sink.astype(jnp.float32).reshape(kvh, g)[None, None, :, :, None, None]
    m = jnp.maximum(jnp.max(s, axis=-1, keepdims=True), sink_l)
    p = jnp.exp(s - m)
    denom = jnp.sum(p, axis=-1, keepdims=True) + jnp.exp(sink_l - m)
    o = jnp.einsum("bnkgqs,bnskd->bnqkgd", (p / denom).astype(v.dtype), vb)
    return o.reshape(b, t, h * d)


def memory_attention(q, km, vm):
    b, t, h, d = q.shape
    s = jnp.einsum("bthd,bmhd->bhtm", q, km, preferred_element_type=jnp.float32)
    p = jax.nn.softmax(s, axis=-1)
    o = jnp.einsum("bhtm,bmhd->bthd", p.astype(vm.dtype), vm)
    return o.reshape(b, t, h * d)


def layer(x, mem, norm_g, w_in, q_norm_g, k_norm_g, rpb, sink, mem_norm_g, w_mem_kv, w_out):
    b, t, _ = x.shape
    scale = HEAD_DIM ** -0.5
    hn = rms_norm(x, norm_g)
    proj = hn @ w_in
    sizes = [D_NA, D_NA, D_NA, D_NA, D_WIN, D_KV_WIN, D_KV_WIN, D_WIN, D_MEM, D_MEM]
    points = []
    acc = 0
    for sz in sizes[:-1]:
        acc += sz
        points.append(acc)
    na_q, na_k, na_v, na_g, wq, wk, wv, wg, mq, mg = jnp.split(proj, points, axis=-1)
    qa = rms_norm(na_q.reshape(b, t, N_HEADS_NA, HEAD_DIM), q_norm_g[0]) * scale
    ka = rms_norm(na_k.reshape(b, t, N_HEADS_NA, HEAD_DIM), k_norm_g[0])
    va = na_v.reshape(b, t, N_HEADS_NA, HEAD_DIM)
    ya = neighbourhood_attention(qa, ka, va, rpb) * jax.nn.silu(na_g)
    qb = rms_norm(wq.reshape(b, t, N_HEADS_WIN, HEAD_DIM), q_norm_g[1]) * scale
    kb = rms_norm(wk.reshape(b, t, N_KV_WIN, HEAD_DIM), k_norm_g[1])
    vb = wv.reshape(b, t, N_KV_WIN, HEAD_DIM)
    yb = windowed_gqa(qb, kb, vb, sink) * jax.nn.silu(wg)
    n_mem = mem.shape[1]
    mkv = rms_norm(mem, mem_norm_g) @ w_mem_kv
    mk, mv = jnp.split(mkv, 2, axis=-1)
    km = rms_norm(mk.reshape(b, n_mem, N_HEADS_MEM, HEAD_DIM), k_norm_g[2])
    vm = mv.reshape(b, n_mem, N_HEADS_MEM, HEAD_DIM)
    qc = rms_norm(mq.reshape(b, t, N_HEADS_MEM, HEAD_DIM), q_norm_g[2]) * scale
    yc = memory_attention(qc, km, vm) * jax.nn.silu(mg)
    y = jnp.concatenate([ya, yb, yc], axis=-1) @ w_out
    return x + y


def setup_inputs(seed: int = 0) -> dict:
    key = jax.random.key(seed)
    ks = jax.random.split(key, 13)
    f32 = jnp.float32
    x_prompt = jax.random.normal(ks[0], (BATCH, SEQ, D_MODEL), f32)
    x_sample = jax.random.normal(ks[1], (DEC_BATCH, DEC_SEQ, D_MODEL), f32)
    mem_prompt = jax.random.normal(ks[2], (BATCH, N_MEM, D_MODEL), f32)
    mem_sample = jax.random.normal(ks[3], (DEC_BATCH, N_MEM, D_MODEL), f32)
    norm_g = 1.0 + 0.02 * jax.random.normal(ks[4], (DEPTH, D_MODEL), f32)
    w_in = jax.random.normal(ks[5], (DEPTH, D_MODEL, D_IN), f32) * D_MODEL ** -0.5
    q_norm_g = 1.0 + 0.02 * jax.random.normal(ks[6], (DEPTH, 3, HEAD_DIM), f32)
    k_norm_g = 1.0 + 0.02 * jax.random.normal(ks[7], (DEPTH, 3, HEAD_DIM), f32)
    rpb = 0.1 * jax.random.normal(ks[8], (DEPTH, N_HEADS_NA, 2 * NA_ROWS_MAX - 1, 2 * NA_COLS - 1), f32)
    sink = 0.5 * jax.random.normal(ks[9], (DEPTH, N_HEADS_WIN), f32)
    mem_norm_g = 1.0 + 0.02 * jax.random.normal(ks[10], (DEPTH, D_MODEL), f32)
    w_mem_kv = jax.random.normal(ks[11], (DEPTH, D_MODEL, 2 * D_MEM), f32) * D_MODEL ** -0.5
    w_out = jax.random.normal(ks[12], (DEPTH, D_MIX, D_MODEL), f32) * D_MIX ** -0.5
    return {"x_prompt": x_prompt, "x_sample": x_sample, "mem_prompt": mem_prompt, "mem_sample": mem_sample,
            "norm_g": norm_g, "w_in": w_in, "q_norm_g": q_norm_g, "k_norm_g": k_norm_g, "rpb": rpb,
            "sink": sink, "mem_norm_g": mem_norm_g, "w_mem_kv": w_mem_kv, "w_out": w_out}


def reference(x_prompt, x_sample, mem_prompt, mem_sample, norm_g, w_in, q_norm_g, k_norm_g, rpb, sink,
              mem_norm_g, w_mem_kv, w_out):
    y_prompt = x_prompt
    y_sample = x_sample
    for l in range(DEPTH):
        y_prompt = layer(y_prompt, mem_prompt, norm_g[l], w_in[l], q_norm_g[l], k_norm_g[l], rpb[l], sink[l],
                         mem_norm_g[l], w_mem_kv[l], w_out[l])
        y_sample = layer(y_sample, mem_sample, norm_g[l], w_in[l], q_norm_g[l], k_norm_g[l], rpb[l], sink[l],
                         mem_norm_g[l], w_mem_kv[l], w_out[l])
    return (y_prompt, y_sample)
```

```python
import functools

import numpy as np
import jax
import jax.numpy as jnp
from jax import lax
from jax.experimental import pallas as pl
from jax.experimental.pallas import tpu as pltpu

D_MODEL = 1024
HEAD_DIM = 64
N_HEADS_NA = 6
N_HEADS_WIN = 6
N_KV_WIN = 2
N_HEADS_MEM = 4
D_NA = N_HEADS_NA * HEAD_DIM
D_WIN = N_HEADS_WIN * HEAD_DIM
D_KV_WIN = N_KV_WIN * HEAD_DIM
D_MEM = N_HEADS_MEM * HEAD_DIM
D_MIX = D_NA + D_WIN + D_MEM
D_K = D_NA + D_KV_WIN
N_MEM = 256
GRID_W = 64
NA_ROWS = 8
NA_COLS = 16
WINDOW = 128
RMS_EPS = 1e-6
NEG_INF = -1e30

TILE = 256
TILE_ROWS = TILE // GRID_W
WIN_G = N_HEADS_WIN // N_KV_WIN
VMEM_LIMIT_BYTES = 56 * 1024 * 1024

_BF16 = jnp.bfloat16
_F32 = jnp.float32
_NT = (((1,), (1,)), ((), ()))


def _rms_rows(x, gain):
    ms = jnp.mean(x * x, axis=-1, keepdims=True)
    return x * lax.rsqrt(ms + RMS_EPS) * gain


def _head_norm_cols(p, gain):
    ms = jnp.mean(p * p, axis=0, keepdims=True)
    return p * lax.rsqrt(ms + RMS_EPS) * gain


def _proj_kernel(x_ref, ng_ref, wt_ref, gq_ref, gk_ref, q_out, k_out, v_out, g_out):
    ta = x_ref.shape[0]
    xn = _rms_rows(x_ref[...], ng_ref[...]).astype(_BF16)

    def project(r0, rows):
        return lax.dot_general(wt_ref[r0:r0 + rows, :], xn, _NT, preferred_element_type=_F32)

    def store_tiles(out_ref, r0, val):
        for j in range(ta // TILE):
            out_ref[j, r0:r0 + val.shape[0], :] = val[:, j * TILE:(j + 1) * TILE]

    for c0 in range(0, D_MIX, 256):
        p = project(c0, 256)
        for hh in range(4):
            r0 = c0 + hh * HEAD_DIM
            qh = _head_norm_cols(p[hh * HEAD_DIM:(hh + 1) * HEAD_DIM, :], gq_ref[r0:r0 + HEAD_DIM, :])
            store_tiles(q_out, r0, qh.astype(_BF16))
    p = project(D_MIX, D_K)
    kn = jnp.concatenate(
        [_head_norm_cols(p[h * HEAD_DIM:(h + 1) * HEAD_DIM, :], gk_ref[h * HEAD_DIM:(h + 1) * HEAD_DIM, :])
         for h in range(D_K // HEAD_DIM)], axis=0)
    k_out[...] = kn.T.astype(_BF16)
    store_tiles(v_out, 0, project(D_MIX + D_K, D_K).astype(_BF16))
    for c0 in range(0, D_MIX, 256):
        p = project(D_MIX + 2 * D_K + c0, 256)
        store_tiles(g_out, c0, (p * jax.nn.sigmoid(p)).astype(_BF16))


def _proj_call(x, ng, wt, gq, gk):
    b, t, _ = x.shape
    ta = 512 if t % 512 == 0 else TILE
    nt = t // TILE
    tpb = ta // TILE
    full = lambda shape: pl.BlockSpec(shape, lambda i, j: (0,) * len(shape))
    return pl.pallas_call(
        _proj_kernel,
        grid=(b, t // ta),
        in_specs=[
            pl.BlockSpec((None, ta, D_MODEL), lambda i, j: (i, j, 0)),
            full((1, D_MODEL)),
            full((D_MIX + 2 * D_K + D_MIX, D_MODEL)),
            full((D_MIX, 1)),
            full((D_K, 1)),
        ],
        out_specs=[
            pl.BlockSpec((None, tpb, D_MIX, TILE), lambda i, j: (i, j, 0, 0)),
            pl.BlockSpec((None, ta, D_K), lambda i, j: (i, j, 0)),
            pl.BlockSpec((None, tpb, D_K, TILE), lambda i, j: (i, j, 0, 0)),
            pl.BlockSpec((None, tpb, D_MIX, TILE), lambda i, j: (i, j, 0, 0)),
        ],
        out_shape=[
            jax.ShapeDtypeStruct((b, nt, D_MIX, TILE), _BF16),
            jax.ShapeDtypeStruct((b, t, D_K), _BF16),
            jax.ShapeDtypeStruct((b, nt, D_K, TILE), _BF16),
            jax.ShapeDtypeStruct((b, nt, D_MIX, TILE), _BF16),
        ],
        compiler_params=pltpu.CompilerParams(
            dimension_semantics=("arbitrary", "arbitrary"), vmem_limit_bytes=VMEM_LIMIT_BYTES),
        name="proj",
    )(x, ng, wt, gq, gk)


def _memkv_kernel(mem_ref, g_ref, wt_ref, gk_ref, km_out, vm_out):
    xn = _rms_rows(mem_ref[...], g_ref[...]).astype(_BF16)
    p = lax.dot_general(wt_ref[...], xn, _NT, preferred_element_type=_F32)
    kn = jnp.concatenate(
        [_head_norm_cols(p[h * HEAD_DIM:(h + 1) * HEAD_DIM, :], gk_ref[h * HEAD_DIM:(h + 1) * HEAD_DIM, :])
         for h in range(N_HEADS_MEM)], axis=0)
    km_out[...] = kn.T.astype(_BF16)
    vm_out[...] = p[D_MEM:, :].astype(_BF16)


def _memkv_call(mem, g, wt, gk):
    bm = mem.shape[0]
    depth = g.shape[0]
    return pl.pallas_call(
        _memkv_kernel,
        grid=(depth, bm),
        in_specs=[
            pl.BlockSpec((None, N_MEM, D_MODEL), lambda l, i: (i, 0, 0)),
            pl.BlockSpec((None, 1, D_MODEL), lambda l, i: (l, 0, 0)),
            pl.BlockSpec((None, 2 * D_MEM, D_MODEL), lambda l, i: (l, 0, 0)),
            pl.BlockSpec((None, D_MEM, 1), lambda l, i: (l, 0, 0)),
        ],
        out_specs=[
            pl.BlockSpec((None, None, N_MEM, D_MEM), lambda l, i: (l, i, 0, 0)),
            pl.BlockSpec((None, None, D_MEM, N_MEM), lambda l, i: (l, i, 0, 0)),
        ],
        out_shape=[
            jax.ShapeDtypeStruct((depth, bm, N_MEM, D_MEM), _BF16),
            jax.ShapeDtypeStruct((depth, bm, D_MEM, N_MEM), _BF16),
        ],
        compiler_params=pltpu.CompilerParams(
            dimension_semantics=("arbitrary", "arbitrary"), vmem_limit_bytes=VMEM_LIMIT_BYTES),
        name="memkv",
    )(mem, g, wt, gk)


def _pad_head(q, odd):
    z = jnp.zeros_like(q)
    return jnp.concatenate([z, q] if odd else [q, z], axis=0)


def _softmax_pv(s_scr, rows, width, m, v_parts, extra=None):
    l = jnp.zeros((1, width), _F32)
    acc = jnp.zeros((HEAD_DIM, width), _F32)
    for v, r0, r in v_parts:
        p = jnp.exp(s_scr[r0:r0 + r, 0:width] - m)
        l = l + jnp.sum(p, axis=0, keepdims=True)
        acc = acc + jnp.dot(v, p.astype(_BF16), preferred_element_type=_F32)
    if extra is not None:
        l = l + jnp.exp(extra - m)
    return acc / l


def _attn_kernel(x_ref, q_ref, g_ref, kp_ref, kc_ref, kn_ref, vp_ref, vc_ref, vn_ref, km_ref, vm_ref,
                 bna_ref, bwin_ref, sink_ref, wot_ref, o_ref, s_scr, mix_scr):
    n = pl.program_id(0)
    n_tiles = pl.num_programs(0)
    k_refs = (kp_ref, kc_ref, kn_ref)
    v_refs = (vp_ref, vc_ref, vn_ref)

    def gate_store(r0, lanes, o):
        gate = g_ref[r0:r0 + HEAD_DIM, lanes].astype(_F32)
        mix_scr[r0:r0 + HEAD_DIM, lanes] = (o * gate).astype(_BF16)

    for h in range(N_HEADS_NA):
        rhs = _pad_head(q_ref[h * HEAD_DIM:(h + 1) * HEAD_DIM, :], h % 2)
        lanes = slice(128 * (h // 2), 128 * (h // 2) + 128)
        m = None
        for j, k_ref in enumerate(k_refs):
            s = jnp.dot(k_ref[:, lanes], rhs, preferred_element_type=_F32)
            s = s + bna_ref[h, j * TILE:(j + 1) * TILE, :]
            s_scr[j * TILE:(j + 1) * TILE, 0:TILE] = s
            mj = jnp.max(s, axis=0, keepdims=True)
            m = mj if m is None else jnp.maximum(m, mj)
        v_parts = [(v_ref[h * HEAD_DIM:(h + 1) * HEAD_DIM, :], j * TILE, TILE) for j, v_ref in enumerate(v_refs)]
        o = _softmax_pv(s_scr, 3 * TILE, TILE, m, v_parts)
        gate_store(h * HEAD_DIM, slice(0, TILE), o)

    wb = WINDOW
    for blk in range(TILE // wb):
        if blk == 0:
            variant = jnp.where(n == 0, 0, 1)
            k_parts = [(kp_ref, wb), (kc_ref, 0), (kc_ref, wb)]
            v_parts_ref = [(vp_ref, wb), (vc_ref, 0), (vc_ref, wb)]
        else:
            variant = jnp.where(n == n_tiles - 1, 2, 1)
            k_parts = [(kc_ref, 0), (kc_ref, wb), (kn_ref, 0)]
            v_parts_ref = [(vc_ref, 0), (vc_ref, wb), (vn_ref, 0)]
        qlanes = slice(blk * wb, (blk + 1) * wb)
        for kv in range(N_KV_WIN):
            q_cat = jnp.concatenate(
                [q_ref[D_NA + (kv * WIN_G + g) * HEAD_DIM:D_NA + (kv * WIN_G + g + 1) * HEAD_DIM, qlanes]
                 for g in range(WIN_G)], axis=1)
            rhs = _pad_head(q_cat, kv % 2)
            width = WIN_G * wb
            m = None
            for j, (k_ref, r0) in enumerate(k_parts):
                s = jnp.dot(k_ref[r0:r0 + wb, D_NA:D_NA + D_KV_WIN], rhs, preferred_element_type=_F32)
                s = s + bwin_ref[variant, kv, j * wb:(j + 1) * wb, :]
                s_scr[j * wb:(j + 1) * wb, 0:width] = s
                mj = jnp.max(s, axis=0, keepdims=True)
                m = mj if m is None else jnp.maximum(m, mj)
            sink = sink_ref[kv]
            m = jnp.maximum(m, sink)
            vrow = D_NA + kv * HEAD_DIM
            v_parts = [(v_ref[vrow:vrow + HEAD_DIM, r0:r0 + wb], j * wb, wb)
                       for j, (v_ref, r0) in enumerate(v_parts_ref)]
            o = _softmax_pv(s_scr, 3 * wb, width, m, v_parts, extra=sink)
            for g in range(WIN_G):
                r0 = D_NA + (kv * WIN_G + g) * HEAD_DIM
                gate = g_ref[r0:r0 + HEAD_DIM, qlanes].astype(_F32)
                mix_scr[r0:r0 + HEAD_DIM, qlanes] = (o[:, g * wb:(g + 1) * wb] * gate).astype(_BF16)

    for h in range(N_HEADS_MEM):
        r0 = D_NA + D_WIN + h * HEAD_DIM
        rhs = _pad_head(q_ref[r0:r0 + HEAD_DIM, :], h % 2)
        lanes = slice(128 * (h // 2), 128 * (h // 2) + 128)
        s = jnp.dot(km_ref[:, lanes], rhs, preferred_element_type=_F32)
        s_scr[0:N_MEM, 0:TILE] = s
        m = jnp.max(s, axis=0, keepdims=True)
        o = _softmax_pv(s_scr, N_MEM, TILE, m, [(vm_ref[h * HEAD_DIM:(h + 1) * HEAD_DIM, :], 0, N_MEM)])
        gate_store(r0, slice(0, TILE), o)

    y_t = jnp.dot(wot_ref[...], mix_scr[...], preferred_element_type=_F32)
    o_ref[...] = x_ref[...] + y_t.T


def _attn_call(x, q, g, k, v, km, vm, bna, bwin, sink_rows, wot):
    b, t, _ = x.shape
    nt = t // TILE
    last = nt - 1
    tile4 = lambda rows: (None, None, rows, TILE)
    return pl.pallas_call(
        _attn_kernel,
        grid=(nt, b),
        in_specs=[
            pl.BlockSpec((None, TILE, D_MODEL), lambda n, i: (i, n, 0)),
            pl.BlockSpec(tile4(D_MIX), lambda n, i: (i, n, 0, 0)),
            pl.BlockSpec(tile4(D_MIX), lambda n, i: (i, n, 0, 0)),
            pl.BlockSpec((None, TILE, D_K), lambda n, i: (i, jnp.maximum(n - 1, 0), 0)),
            pl.BlockSpec((None, TILE, D_K), lambda n, i: (i, n, 0)),
            pl.BlockSpec((None, TILE, D_K), lambda n, i: (i, jnp.minimum(n + 1, last), 0)),
            pl.BlockSpec(tile4(D_K), lambda n, i: (i, jnp.maximum(n - 1, 0), 0, 0)),
            pl.BlockSpec(tile4(D_K), lambda n, i: (i, n, 0, 0)),
            pl.BlockSpec(tile4(D_K), lambda n, i: (i, jnp.minimum(n + 1, last), 0, 0)),
            pl.BlockSpec((None, N_MEM, D_MEM), lambda n, i: (i, 0, 0)),
            pl.BlockSpec((None, D_MEM, N_MEM), lambda n, i: (i, 0, 0)),
            pl.BlockSpec((None, N_HEADS_NA, 3 * TILE, TILE),
                         lambda n, i: (jnp.where(n == 0, 0, jnp.where(n == last, 2, 1)), 0, 0, 0)),
            pl.BlockSpec((3, N_KV_WIN, 3 * WINDOW, WIN_G * WINDOW), lambda n, i: (0, 0, 0, 0)),
            pl.BlockSpec((N_KV_WIN, 1, WIN_G * WINDOW), lambda n, i: (0, 0, 0)),
            pl.BlockSpec((D_MODEL, D_MIX), lambda n, i: (0, 0)),
        ],
        out_specs=pl.BlockSpec((None, TILE, D_MODEL), lambda n, i: (i, n, 0)),
        out_shape=jax.ShapeDtypeStruct(x.shape, x.dtype),
        scratch_shapes=[
            pltpu.VMEM((3 * TILE, WIN_G * WINDOW), _F32),
            pltpu.VMEM((D_MIX, TILE), _BF16),
        ],
        compiler_params=pltpu.CompilerParams(
            dimension_semantics=("arbitrary", "arbitrary"), vmem_limit_bytes=VMEM_LIMIT_BYTES),
        name="attn",
    )(x, q, g, k, k, k, v, v, v, km, vm, bna, bwin, sink_rows, wot)


def _na_index_tables(n_tiles):
    rows = n_tiles * TILE_ROWS
    rel_rows, row_ok = [], []
    for n in (0, 1, n_tiles - 1):
        qr = n * TILE_ROWS + np.arange(TILE_ROWS)
        kr = (n - 1) * TILE_ROWS + np.arange(3 * TILE_ROWS)
        r0 = np.clip(qr - NA_ROWS // 2, 0, rows - NA_ROWS)
        ok = (kr[:, None] >= r0[None, :]) & (kr[:, None] < r0[None, :] + NA_ROWS) & (kr[:, None] >= 0) & (kr[:, None] < rows)
        rel_rows.append(np.clip(kr[:, None] - qr[None, :] + NA_ROWS - 1, 0, 2 * NA_ROWS - 2))
        row_ok.append(ok)
    qc = np.arange(GRID_W)
    kc = np.arange(GRID_W)
    cs = np.clip(qc - NA_COLS // 2, 0, GRID_W - NA_COLS)
    col_ok = (kc[:, None] >= cs[None, :]) & (kc[:, None] < cs[None, :] + NA_COLS)
    rel_col = np.clip(kc[:, None] - qc[None, :] + NA_COLS - 1, 0, 2 * NA_COLS - 2)
    rel_rows = np.stack(rel_rows)
    row_ok = np.stack(row_ok)
    valid = row_ok[:, :, None, :, None] & col_ok[None, None, :, None, :]
    return rel_rows.astype(np.int32), rel_col.astype(np.int32), valid.reshape(3, 3 * TILE, TILE)


def _na_bias(rpb, n_tiles):
    rel_rows, rel_col, valid = _na_index_tables(n_tiles)
    t1 = jnp.take(rpb, jnp.asarray(rel_col.reshape(-1)), axis=2).reshape(N_HEADS_NA, 2 * NA_ROWS - 1, GRID_W, GRID_W)
    t2 = jnp.take(t1, jnp.asarray(rel_rows.reshape(-1)), axis=1)
    t2 = t2.reshape(N_HEADS_NA, 3, 3 * TILE_ROWS, TILE_ROWS, GRID_W, GRID_W)
    t2 = jnp.transpose(t2, (1, 0, 2, 4, 3, 5)).reshape(3, N_HEADS_NA, 3 * TILE, TILE)
    return jnp.where(jnp.asarray(valid)[:, None], t2, NEG_INF).astype(_F32)


def _win_bias():
    i = np.arange(WINDOW)
    w = np.arange(3 * WINDOW)
    dist = (i[None, :] - w[:, None] + WINDOW).astype(np.float32)
    band = np.abs(dist) <= WINDOW
    edge = np.stack([w >= WINDOW, w >= 0, w < 2 * WINDOW])
    valid = band[None] & edge[:, :, None]
    slopes = 2.0 ** (-8.0 * jnp.arange(1, N_HEADS_WIN + 1, dtype=_F32) / N_HEADS_WIN)
    ali = -slopes.reshape(N_KV_WIN, WIN_G)[:, None, :, None] * jnp.abs(jnp.asarray(dist))[None, :, None, :]
    bias = jnp.where(jnp.asarray(valid)[:, None, :, None, :], ali[None], NEG_INF)
    return bias.reshape(3, N_KV_WIN, 3 * WINDOW, WIN_G * WINDOW).astype(_F32)


def _prep_weights(norm_g, w_in, q_norm_g, k_norm_g, sink, mem_norm_g, w_mem_kv, w_out):
    depth = w_in.shape[0]
    sizes = [D_NA, D_NA, D_NA, D_NA, D_WIN, D_KV_WIN, D_KV_WIN, D_WIN, D_MEM, D_MEM]
    offs = np.concatenate([[0], np.cumsum(sizes)])
    seg = lambda i: w_in[:, :, offs[i]:offs[i + 1]]
    na_q, na_k, na_v, na_g, wq, wk, wv, wg, mq, mg = [seg(i) for i in range(10)]
    w_perm = jnp.concatenate([na_q, wq, mq, na_k, wk, na_v, wv, na_g, wg, mg], axis=-1)
    wt = jnp.swapaxes(w_perm, 1, 2).astype(_BF16)
    scale = HEAD_DIM ** -0.5
    gq = jnp.concatenate([jnp.tile(q_norm_g[:, 0], (1, N_HEADS_NA)), jnp.tile(q_norm_g[:, 1], (1, N_HEADS_WIN)),
                          jnp.tile(q_norm_g[:, 2], (1, N_HEADS_MEM))], axis=-1) * scale
    gk = jnp.concatenate([jnp.tile(k_norm_g[:, 0], (1, N_HEADS_NA)), jnp.tile(k_norm_g[:, 1], (1, N_KV_WIN))], axis=-1)
    gkm = jnp.tile(k_norm_g[:, 2], (1, N_HEADS_MEM))
    sink_rows = jnp.repeat(sink.astype(_F32).reshape(depth, N_KV_WIN, 1, WIN_G), WINDOW, axis=-1)
    return dict(
        ng=norm_g.reshape(depth, 1, D_MODEL), wt=wt,
        gq=gq.reshape(depth, D_MIX, 1), gk=gk.reshape(depth, D_K, 1), gkm=gkm.reshape(depth, D_MEM, 1),
        sink_rows=sink_rows,
        mg=mem_norm_g.reshape(depth, 1, D_MODEL),
        wmt=jnp.swapaxes(w_mem_kv, 1, 2).astype(_BF16),
        wot=jnp.swapaxes(w_out, 1, 2).astype(_BF16),
    )


def _forward(xs, mems, norm_g, w_in, q_norm_g, k_norm_g, rpb, sink, mem_norm_g, w_mem_kv, w_out):
    depth = w_in.shape[0]
    w = _prep_weights(norm_g, w_in, q_norm_g, k_norm_g, sink, mem_norm_g, w_mem_kv, w_out)
    km, vm = _memkv_call(jnp.concatenate(mems, axis=0), w["mg"], w["wmt"], w["gkm"])
    bwin = _win_bias()
    mem_off = np.concatenate([[0], np.cumsum([m.shape[0] for m in mems])])
    xs = list(xs)
    for l in range(depth):
        bna = {}
        for gi, x in enumerate(xs):
            nt = x.shape[1] // TILE
            if nt not in bna:
                bna[nt] = _na_bias(rpb[l], nt)
            q, k, v, g = _proj_call(x, w["ng"][l], w["wt"][l], w["gq"][l], w["gk"][l])
            lo, hi = mem_off[gi], mem_off[gi + 1]
            xs[gi] = _attn_call(x, q, g, k, v, km[l, lo:hi], vm[l, lo:hi], bna[nt], bwin,
                                w["sink_rows"][l], w["wot"][l])
    return tuple(xs)


def kernel(x_prompt, x_sample, mem_prompt, mem_sample, norm_g, w_in, q_norm_g, k_norm_g, rpb, sink, mem_norm_g, w_mem_kv, w_out):
    return _forward((x_prompt, x_sample), (mem_prompt, mem_sample), norm_g, w_in, q_norm_g, k_norm_g, rpb, sink,
                    mem_norm_g, w_mem_kv, w_out)
```

```python
import functools

import numpy as np
import jax
import jax.numpy as jnp
from jax import lax
from jax.experimental import pallas as pl
from jax.experimental.pallas import tpu as pltpu

D_MODEL = 1024
HEAD_DIM = 64
N_HEADS_NA = 6
N_HEADS_WIN = 6
N_KV_WIN = 2
N_HEADS_MEM = 4
N_HEADS = N_HEADS_NA + N_HEADS_WIN + N_HEADS_MEM
D_NA = N_HEADS_NA * HEAD_DIM
D_WIN = N_HEADS_WIN * HEAD_DIM
D_KV_WIN = N_KV_WIN * HEAD_DIM
D_MEM = N_HEADS_MEM * HEAD_DIM
D_MIX = D_NA + D_WIN + D_MEM
D_K = D_NA + D_KV_WIN
D_PROJ = D_MIX + 2 * D_K + D_MIX
N_MEM = 256
GRID_W = 64
NA_ROWS = 8
NA_COLS = 16
WINDOW = 128
RMS_EPS = 1e-6
NEG_INF = -1e30
LOG2E = 1.4426950408889634
MAX_SHIFT_SLACK = 64.0

TILE = 256
TILE_ROWS = TILE // GRID_W
WIN_G = N_HEADS_WIN // N_KV_WIN
WIN_LANES = N_KV_WIN * WIN_G * WINDOW
VMEM_LIMIT_BYTES = 56 * 1024 * 1024

_BF16 = jnp.bfloat16
_F32 = jnp.float32
_NT = (((1,), (1,)), ((), ()))
_TN = (((0,), (0,)), ((), ()))


def _rms_rows(x, gain):
    ms = jnp.mean(x * x, axis=-1, keepdims=True)
    return x * lax.rsqrt(ms + RMS_EPS) * gain


def _head_norm_cols(p, gain):
    ms = jnp.mean(p * p, axis=0, keepdims=True)
    return p * lax.rsqrt(ms + RMS_EPS) * gain


def _proj_kernel(x_ref, ng_ref, wt_ref, gq_ref, gk_ref, kb_ref, bm_ref, fl_ref, q_out, k_out, v_out, g_out, nm_out):
    ta = x_ref.shape[0]
    xn = _rms_rows(x_ref[...], ng_ref[...]).astype(_BF16)

    def project(r0, rows):
        return lax.dot_general(wt_ref[r0:r0 + rows, :], xn, _NT, preferred_element_type=_F32)

    def store_tiles(out_ref, r0, val):
        for j in range(ta // TILE):
            out_ref[j, r0:r0 + val.shape[0], :] = val[:, j * TILE:(j + 1) * TILE]

    for c0 in range(0, D_MIX, 256):
        p = project(c0, 256)
        for hh in range(4):
            r0 = c0 + hh * HEAD_DIM
            h = r0 // HEAD_DIM
            qh = _head_norm_cols(p[hh * HEAD_DIM:(hh + 1) * HEAD_DIM, :], gq_ref[r0:r0 + HEAD_DIM, :]).astype(_BF16)
            store_tiles(q_out, r0, qh)
            qf = qh.astype(_F32)
            bound = jnp.sqrt(jnp.sum(qf * qf, axis=0, keepdims=True)) * kb_ref[h:h + 1, :] + bm_ref[h:h + 1, :]
            store_tiles(nm_out, h, -jnp.maximum(bound, fl_ref[h:h + 1, :]))
    p = project(D_MIX, D_K)
    kn = jnp.concatenate(
        [_head_norm_cols(p[h * HEAD_DIM:(h + 1) * HEAD_DIM, :], gk_ref[h * HEAD_DIM:(h + 1) * HEAD_DIM, :])
         for h in range(D_K // HEAD_DIM)], axis=0)
    k_out[...] = kn.T.astype(_BF16)
    store_tiles(v_out, 0, project(D_MIX + D_K, D_K).astype(_BF16))
    for c0 in range(0, D_MIX, 256):
        p = project(D_MIX + 2 * D_K + c0, 256)
        store_tiles(g_out, c0, (p * jax.nn.sigmoid(p)).astype(_BF16))


def _proj_call(layer, x, w):
    b, t, _ = x.shape
    ta = 512 if t % 512 == 0 else TILE
    nt = t // TILE
    tpb = ta // TILE
    per_layer = lambda *shape: pl.BlockSpec((None,) + shape, lambda i, j: (layer,) + (0,) * len(shape))
    tiles = lambda rows: pl.BlockSpec((None, tpb, rows, TILE), lambda i, j: (i, j, 0, 0))
    return pl.pallas_call(
        _proj_kernel,
        grid=(b, t // ta),
        in_specs=[
            pl.BlockSpec((None, ta, D_MODEL), lambda i, j: (i, j, 0)),
            per_layer(1, D_MODEL),
            per_layer(D_PROJ, D_MODEL),
            per_layer(D_MIX, 1),
            per_layer(D_K, 1),
            per_layer(N_HEADS, 1),
            per_layer(N_HEADS, 1),
            per_layer(N_HEADS, 1),
        ],
        out_specs=[
            tiles(D_MIX),
            pl.BlockSpec((None, ta, D_K), lambda i, j: (i, j, 0)),
            tiles(D_K),
            tiles(D_MIX),
            tiles(N_HEADS),
        ],
        out_shape=[
            jax.ShapeDtypeStruct((b, nt, D_MIX, TILE), _BF16),
            jax.ShapeDtypeStruct((b, t, D_K), _BF16),
            jax.ShapeDtypeStruct((b, nt, D_K, TILE), _BF16),
            jax.ShapeDtypeStruct((b, nt, D_MIX, TILE), _BF16),
            jax.ShapeDtypeStruct((b, nt, N_HEADS, TILE), _F32),
        ],
        compiler_params=pltpu.CompilerParams(
            dimension_semantics=("arbitrary", "arbitrary"), vmem_limit_bytes=VMEM_LIMIT_BYTES),
        name="proj",
    )(x, w["ng"], w["wt"], w["gq"], w["gk"], w["kb"], w["bm"], w["fl"])


def _memkv_kernel(mem_ref, g_ref, wt_ref, gk_ref, km_out, vm_out):
    xn = _rms_rows(mem_ref[...], g_ref[...]).astype(_BF16)
    p = lax.dot_general(wt_ref[...], xn, _NT, preferred_element_type=_F32)
    kn = jnp.concatenate(
        [_head_norm_cols(p[h * HEAD_DIM:(h + 1) * HEAD_DIM, :], gk_ref[h * HEAD_DIM:(h + 1) * HEAD_DIM, :])
         for h in range(N_HEADS_MEM)], axis=0)
    km_out[...] = kn.T.astype(_BF16)
    vm_out[...] = p[D_MEM:, :].astype(_BF16)


def _memkv_call(mem, g, wt, gk):
    bm = mem.shape[0]
    depth = g.shape[0]
    return pl.pallas_call(
        _memkv_kernel,
        grid=(depth, bm),
        in_specs=[
            pl.BlockSpec((None, N_MEM, D_MODEL), lambda l, i: (i, 0, 0)),
            pl.BlockSpec((None, 1, D_MODEL), lambda l, i: (l, 0, 0)),
            pl.BlockSpec((None, 2 * D_MEM, D_MODEL), lambda l, i: (l, 0, 0)),
            pl.BlockSpec((None, D_MEM, 1), lambda l, i: (l, 0, 0)),
        ],
        out_specs=[
            pl.BlockSpec((None, None, N_MEM, D_MEM), lambda l, i: (l, i, 0, 0)),
            pl.BlockSpec((None, None, D_MEM, N_MEM), lambda l, i: (l, i, 0, 0)),
        ],
        out_shape=[
            jax.ShapeDtypeStruct((depth, bm, N_MEM, D_MEM), _BF16),
            jax.ShapeDtypeStruct((depth, bm, D_MEM, N_MEM), _BF16),
        ],
        compiler_params=pltpu.CompilerParams(
            dimension_semantics=("arbitrary", "arbitrary"), vmem_limit_bytes=VMEM_LIMIT_BYTES),
        name="memkv",
    )(mem, g, wt, gk)


def _pad_head(q, odd):
    z = jnp.zeros_like(q)
    return jnp.concatenate([z, q] if odd else [q, z], axis=0)


class _Unit:
    def __init__(self, k_parts, rhs, bias_of, shift, v_groups, sink, done):
        self.k_parts, self.rhs, self.bias_of, self.shift = k_parts, rhs, bias_of, shift
        self.v_groups, self.sink, self.done = v_groups, sink, done
        self.rows = [0]
        self._rhs = self._shift = self.l = None
        self.accs = [None] * len(v_groups)

    def scores(self, j):
        if self._rhs is None:
            self._rhs = self.rhs()
            self._shift = None if self.shift is None else self.shift()
        k_blk = self.k_parts[j]()
        r0, r = self.rows[j], k_blk.shape[0]
        if len(self.rows) == j + 1:
            self.rows.append(r0 + r)
        s = jnp.dot(k_blk, self._rhs, preferred_element_type=_F32) + self.bias_of(r0, r)
        return s if self._shift is None else s + self._shift

    def probs(self, s):
        p = jnp.exp2(s)
        lj = jnp.sum(p, axis=0, keepdims=True)
        self.l = lj if self.l is None else self.l + lj
        return p.astype(_BF16)

    def weigh(self, j, pb):
        for gi, (ls, vs) in enumerate(self.v_groups):
            d = jnp.dot(vs[j](), pb[:, ls], preferred_element_type=_F32)
            self.accs[gi] = d if self.accs[gi] is None else self.accs[gi] + d

    def finish(self, shift=None):
        shift = self._shift if shift is None else shift
        l = self.l if self.sink is None else self.l + jnp.exp2(self.sink() + shift)
        inv = 1.0 / l
        self.done([acc * inv[:, ls] for acc, (ls, _) in zip(self.accs, self.v_groups)])

    def run_exact(self, scr):
        m = None
        for j in range(len(self.k_parts)):
            s = self.scores(j)
            scr[self.rows[j]:self.rows[j + 1], :] = s
            mj = jnp.max(s, axis=0, keepdims=True)
            m = mj if m is None else jnp.maximum(m, mj)
        if self.sink is not None:
            m = jnp.maximum(m, self.sink())
        for j in range(len(self.k_parts)):
            self.weigh(j, self.probs(scr[self.rows[j]:self.rows[j + 1], :] - m))
        self.finish(-m)


def _attn_kernel(exact_max, x_ref, q_ref, g_ref, nm_ref, kp_ref, kc_ref, kn_ref, vp_ref, vc_ref, vn_ref, km_ref,
                 vm_ref, bna_ref, bwin_ref, sink_ref, wo_ref, o_ref, mix_scr, *score_scr):
    n = pl.program_id(0)
    n_tiles = pl.num_programs(0)
    k_refs = (kp_ref, kc_ref, kn_ref)
    v_refs = (vp_ref, vc_ref, vn_ref)
    wb = WINDOW
    full = slice(0, TILE)
    scr_na, scr_win = score_scr if exact_max else (None, None)

    def gate_store(r0, lanes, o):
        gate = g_ref[r0:r0 + HEAD_DIM, lanes].astype(_F32)
        mix_scr[r0:r0 + HEAD_DIM, lanes] = (o * gate).astype(_BF16)

    def shift_of(h, lanes):
        return None if exact_max else (lambda: nm_ref[h:h + 1, lanes])

    def rows_of(ref, rows, lanes):
        return lambda: ref[rows, lanes]

    def na_unit(h):
        hrows = slice(h * HEAD_DIM, (h + 1) * HEAD_DIM)
        lanes = slice(128 * (h // 2), 128 * (h // 2) + 128)
        return _Unit(
            [rows_of(k_ref, full, lanes) for k_ref in k_refs],
            lambda: _pad_head(q_ref[hrows, :], h % 2),
            lambda r0, r: bna_ref[h, r0:r0 + r, :], shift_of(h, full),
            [(full, [rows_of(v_ref, hrows, full) for v_ref in v_refs])], None,
            lambda outs: gate_store(h * HEAD_DIM, full, outs[0]))

    def win_unit(blk):
        qlanes = slice(blk * wb, (blk + 1) * wb)
        if blk == 0:
            variant = jnp.where(n == 0, 0, 1)
            src = [(0, wb), (1, 0), (1, wb)]
        else:
            variant = jnp.where(n == n_tiles - 1, 2, 1)
            src = [(1, 0), (1, wb), (2, 0)]

        def rhs():
            rows = []
            for kv in range(N_KV_WIN):
                q_cat = jnp.concatenate(
                    [q_ref[D_NA + (kv * WIN_G + g) * HEAD_DIM:D_NA + (kv * WIN_G + g + 1) * HEAD_DIM, qlanes]
                     for g in range(WIN_G)], axis=1)
                z = jnp.zeros_like(q_cat)
                rows.append(jnp.concatenate([z, q_cat] if kv else [q_cat, z], axis=1))
            return jnp.concatenate(rows, axis=0)

        def shift():
            return jnp.concatenate(
                [nm_ref[N_HEADS_NA + hh:N_HEADS_NA + hh + 1, qlanes] for hh in range(N_HEADS_WIN)], axis=1)

        def done(outs):
            for kv, o in enumerate(outs):
                for g in range(WIN_G):
                    gate_store(D_NA + (kv * WIN_G + g) * HEAD_DIM, qlanes, o[:, g * wb:(g + 1) * wb])

        groups = []
        for kv in range(N_KV_WIN):
            vrows = slice(D_NA + kv * HEAD_DIM, D_NA + (kv + 1) * HEAD_DIM)
            vs = [rows_of(v_refs[t], vrows, slice(r0, r0 + wb)) for t, r0 in src]
            groups.append((slice(kv * WIN_G * wb, (kv + 1) * WIN_G * wb), vs))
        return _Unit(
            [rows_of(k_refs[t], slice(r0, r0 + wb), slice(D_NA, D_NA + D_KV_WIN)) for t, r0 in src],
            rhs, lambda r0, r: bwin_ref[variant, r0:r0 + r, :], None if exact_max else shift,
            groups, lambda: sink_ref[...], done)

    def mem_unit(h):
        r0 = D_NA + D_WIN + h * HEAD_DIM
        lanes = slice(128 * (h // 2), 128 * (h // 2) + 128)
        return _Unit(
            [rows_of(km_ref, slice(0, N_MEM), lanes)],
            lambda: _pad_head(q_ref[r0:r0 + HEAD_DIM, :], h % 2),
            lambda r0_, r: 0.0, shift_of(N_HEADS_NA + N_HEADS_WIN + h, full),
            [(full, [rows_of(vm_ref, slice(h * HEAD_DIM, (h + 1) * HEAD_DIM), slice(0, N_MEM))])], None,
            lambda outs: gate_store(r0, full, outs[0]))

    def out_pieces(c, first):
        cs = slice(c * 256, (c + 1) * 256)
        holder = []

        def piece(nb):
            if not holder:
                holder.append(mix_scr[cs, :].T)
            cols = slice(nb * 256, (nb + 1) * 256)
            y = jnp.dot(holder[0], wo_ref[cs, cols], preferred_element_type=_F32)
            o_ref[:, cols] = (x_ref[:, cols] if first else o_ref[:, cols]) + y

        return [functools.partial(piece, nb) for nb in range(D_MODEL // 256)]

    units = ([mem_unit(h) for h in range(N_HEADS_MEM)] + [win_unit(b) for b in range(TILE // wb)]
             + [na_unit(h) for h in range(N_HEADS_NA)])
    chunks_ready_after = {3: [3], 5: [2], 9: [0], 11: [1]}
    first_chunk = chunks_ready_after[min(chunks_ready_after)][0]
    pending = []

    def after_unit(ui):
        for c in chunks_ready_after.get(ui, []):
            pending.extend(out_pieces(c, c == first_chunk))

    if exact_max:
        for ui, unit in enumerate(units):
            unit.run_exact(scr_win if unit.sink is not None else scr_na)
            after_unit(ui)
            while pending:
                pending.pop(0)()
        return

    blocks = [(ui, j) for ui, unit in enumerate(units) for j in range(len(unit.k_parts))]
    s_q, p_q = {}, {}
    for t in range(len(blocks) + 2):
        if t < len(blocks):
            ui, j = blocks[t]
            s_q[t] = units[ui].scores(j)
        if pending:
            pending.pop(0)()
        if 0 <= t - 1 < len(blocks):
            ui, j = blocks[t - 1]
            p_q[t - 1] = units[ui].probs(s_q.pop(t - 1))
        if 0 <= t - 2 < len(blocks):
            ui, j = blocks[t - 2]
            units[ui].weigh(j, p_q.pop(t - 2))
            if j == len(units[ui].k_parts) - 1:
                units[ui].finish()
                after_unit(ui)
    while pending:
        pending.pop(0)()


def _attn_call(layer, exact_max, x, q, g, nm, k, v, km, vm, mem_lo, bna, bwin, sink_rows, wo):
    b, t, _ = x.shape
    nt = t // TILE
    last = nt - 1
    tile4 = lambda rows: (None, None, rows, TILE)
    scratch = [pltpu.VMEM((D_MIX, TILE), _BF16)]
    if exact_max:
        scratch += [pltpu.VMEM((3 * TILE, TILE), _F32), pltpu.VMEM((3 * WINDOW, WIN_LANES), _F32)]
    return pl.pallas_call(
        functools.partial(_attn_kernel, exact_max),
        grid=(nt, b),
        in_specs=[
            pl.BlockSpec((None, TILE, D_MODEL), lambda n, i: (i, n, 0)),
            pl.BlockSpec(tile4(D_MIX), lambda n, i: (i, n, 0, 0)),
            pl.BlockSpec(tile4(D_MIX), lambda n, i: (i, n, 0, 0)),
            pl.BlockSpec(tile4(N_HEADS), lambda n, i: (i, n, 0, 0)),
            pl.BlockSpec((None, TILE, D_K), lambda n, i: (i, jnp.maximum(n - 1, 0), 0)),
            pl.BlockSpec((None, TILE, D_K), lambda n, i: (i, n, 0)),
            pl.BlockSpec((None, TILE, D_K), lambda n, i: (i, jnp.minimum(n + 1, last), 0)),
            pl.BlockSpec(tile4(D_K), lambda n, i: (i, jnp.maximum(n - 1, 0), 0, 0)),
            pl.BlockSpec(tile4(D_K), lambda n, i: (i, n, 0, 0)),
            pl.BlockSpec(tile4(D_K), lambda n, i: (i, jnp.minimum(n + 1, last), 0, 0)),
            pl.BlockSpec((None, None, N_MEM, D_MEM), lambda n, i: (layer, mem_lo + i, 0, 0)),
            pl.BlockSpec((None, None, D_MEM, N_MEM), lambda n, i: (layer, mem_lo + i, 0, 0)),
            pl.BlockSpec((None, None, N_HEADS_NA, 3 * TILE, TILE),
                         lambda n, i: (layer, jnp.where(n == 0, 0, jnp.where(n == last, 2, 1)), 0, 0, 0)),
            pl.BlockSpec((3, 3 * WINDOW, WIN_LANES), lambda n, i: (0, 0, 0)),
            pl.BlockSpec((None, 1, WIN_LANES), lambda n, i: (layer, 0, 0)),
            pl.BlockSpec((None, D_MIX, D_MODEL), lambda n, i: (layer, 0, 0)),
        ],
        out_specs=pl.BlockSpec((None, TILE, D_MODEL), lambda n, i: (i, n, 0)),
        out_shape=jax.ShapeDtypeStruct(x.shape, x.dtype),
        scratch_shapes=scratch,
        compiler_params=pltpu.CompilerParams(
            dimension_semantics=("arbitrary", "arbitrary"), vmem_limit_bytes=VMEM_LIMIT_BYTES),
        name="attn_exact" if exact_max else "attn",
    )(x, q, g, nm, k, k, k, v, v, v, km, vm, bna, bwin, sink_rows, wo)


def _na_index_tables():
    n_tiles = 4
    rows = n_tiles * TILE_ROWS
    rel_rows, row_ok = [], []
    for n in (0, 1, n_tiles - 1):
        qr = n * TILE_ROWS + np.arange(TILE_ROWS)
        kr = (n - 1) * TILE_ROWS + np.arange(3 * TILE_ROWS)
        r0 = np.clip(qr - NA_ROWS // 2, 0, rows - NA_ROWS)
        ok = ((kr[:, None] >= r0[None, :]) & (kr[:, None] < r0[None, :] + NA_ROWS)
              & (kr[:, None] >= 0) & (kr[:, None] < rows))
        rel_rows.append(np.clip(kr[:, None] - qr[None, :] + NA_ROWS - 1, 0, 2 * NA_ROWS - 2))
        row_ok.append(ok)
    qc = np.arange(GRID_W)
    kc = np.arange(GRID_W)
    cs = np.clip(qc - NA_COLS // 2, 0, GRID_W - NA_COLS)
    col_ok = (kc[:, None] >= cs[None, :]) & (kc[:, None] < cs[None, :] + NA_COLS)
    rel_col = np.clip(kc[:, None] - qc[None, :] + NA_COLS - 1, 0, 2 * NA_COLS - 2)
    rel_rows = np.stack(rel_rows)
    row_ok = np.stack(row_ok)
    valid = row_ok[:, :, None, :, None] & col_ok[None, None, :, None, :]
    return rel_rows.astype(np.int32), rel_col.astype(np.int32), valid.reshape(3, 3 * TILE, TILE)


def _na_bias(rpb):
    depth = rpb.shape[0]
    rel_rows, rel_col, valid = _na_index_tables()
    t1 = jnp.take(rpb * LOG2E, jnp.asarray(rel_col.reshape(-1)), axis=3)
    t1 = t1.reshape(depth, N_HEADS_NA, 2 * NA_ROWS - 1, GRID_W, GRID_W)
    t2 = jnp.take(t1, jnp.asarray(rel_rows.reshape(-1)), axis=2)
    t2 = t2.reshape(depth, N_HEADS_NA, 3, 3 * TILE_ROWS, TILE_ROWS, GRID_W, GRID_W)
    t2 = jnp.transpose(t2, (0, 2, 1, 3, 5, 4, 6)).reshape(depth, 3, N_HEADS_NA, 3 * TILE, TILE)
    return jnp.where(jnp.asarray(valid)[None, :, None], t2, NEG_INF).astype(_F32)


def _win_bias():
    i = np.arange(WINDOW)
    w = np.arange(3 * WINDOW)
    dist = (i[None, :] - w[:, None] + WINDOW).astype(np.float32)
    band = np.abs(dist) <= WINDOW
    edge = np.stack([w >= WINDOW, w >= 0, w < 2 * WINDOW])
    valid = band[None] & edge[:, :, None]
    slopes = 2.0 ** (-8.0 * jnp.arange(1, N_HEADS_WIN + 1, dtype=_F32) / N_HEADS_WIN)
    ali = -slopes[None, :, None] * jnp.abs(jnp.asarray(dist))[:, None, :] * LOG2E
    bias = jnp.where(jnp.asarray(valid)[:, :, None, :], ali[None], NEG_INF)
    return bias.reshape(3, 3 * WINDOW, WIN_LANES).astype(_F32)


def _prep_weights(norm_g, w_in, q_norm_g, k_norm_g, rpb, sink, mem_norm_g, w_mem_kv, w_out):
    depth = w_in.shape[0]
    sizes = [D_NA, D_NA, D_NA, D_NA, D_WIN, D_KV_WIN, D_KV_WIN, D_WIN, D_MEM, D_MEM]
    offs = np.concatenate([[0], np.cumsum(sizes)])
    seg = lambda i: w_in[:, :, offs[i]:offs[i + 1]]
    na_q, na_k, na_v, na_g, wq, wk, wv, wg, mq, mg = [seg(i) for i in range(10)]
    w_perm = jnp.concatenate([na_q, wq, mq, na_k, wk, na_v, wv, na_g, wg, mg], axis=-1)
    wt = jnp.swapaxes(w_perm, 1, 2).astype(_BF16)
    q_scale = HEAD_DIM ** -0.5 * LOG2E
    heads = (N_HEADS_NA, N_HEADS_WIN, N_HEADS_MEM)
    gq = jnp.concatenate([jnp.tile(q_norm_g[:, i], (1, n)) for i, n in enumerate(heads)], axis=-1) * q_scale
    gk = jnp.concatenate([jnp.tile(k_norm_g[:, 0], (1, N_HEADS_NA)), jnp.tile(k_norm_g[:, 1], (1, N_KV_WIN))], axis=-1)
    gkm = jnp.tile(k_norm_g[:, 2], (1, N_HEADS_MEM))
    sink2 = sink.astype(_F32) * LOG2E

    k_bound = HEAD_DIM ** 0.5 * jnp.max(jnp.abs(k_norm_g), axis=-1)
    kb = jnp.concatenate([jnp.repeat(k_bound[:, i:i + 1], n, axis=1) for i, n in enumerate(heads)], axis=1)
    rpb2 = rpb * LOG2E
    bm = jnp.concatenate([jnp.max(rpb2, axis=(2, 3)), jnp.zeros((depth, N_HEADS_WIN + N_HEADS_MEM), _F32)], axis=1)
    fl = jnp.concatenate([jnp.full((depth, N_HEADS_NA), NEG_INF, _F32), sink2,
                          jnp.full((depth, N_HEADS_MEM), NEG_INF, _F32)], axis=1)
    q_bound = HEAD_DIM ** 0.5 * jnp.max(jnp.abs(q_norm_g), axis=-1) * q_scale
    spread = jnp.stack([jnp.max(jnp.max(rpb2, axis=(2, 3)) - jnp.min(rpb2, axis=(2, 3)), axis=1),
                        jnp.zeros((depth,), _F32), jnp.zeros((depth,), _F32)], axis=1)
    slack = jnp.max(2.0 * q_bound * k_bound + spread, axis=1)
    return dict(
        ng=norm_g.reshape(depth, 1, D_MODEL), wt=wt,
        gq=gq.reshape(depth, D_MIX, 1), gk=gk.reshape(depth, D_K, 1), gkm=gkm.reshape(depth, D_MEM, 1),
        kb=kb.reshape(depth, N_HEADS, 1), bm=bm.reshape(depth, N_HEADS, 1), fl=fl.reshape(depth, N_HEADS, 1),
        sink_rows=jnp.repeat(sink2, WINDOW, axis=-1).reshape(depth, 1, WIN_LANES),
        mg=mem_norm_g.reshape(depth, 1, D_MODEL),
        wmt=jnp.swapaxes(w_mem_kv, 1, 2).astype(_BF16),
        wo=w_out.astype(_BF16),
        bound_ok=slack <= MAX_SHIFT_SLACK,
    )


def _forward(xs, mems, norm_g, w_in, q_norm_g, k_norm_g, rpb, sink, mem_norm_g, w_mem_kv, w_out):
    depth = w_in.shape[0]
    w = _prep_weights(norm_g, w_in, q_norm_g, k_norm_g, rpb, sink, mem_norm_g, w_mem_kv, w_out)
    km, vm = _memkv_call(jnp.concatenate(mems, axis=0), w["mg"], w["wmt"], w["gkm"])
    bna = _na_bias(rpb)
    bwin = _win_bias()
    mem_off = np.concatenate([[0], np.cumsum([m.shape[0] for m in mems])])
    xs = list(xs)
    for l in range(depth):
        for gi, x in enumerate(xs):
            q, k, v, g, nm = _proj_call(l, x, w)
            args = (x, q, g, nm, k, v, km, vm, int(mem_off[gi]), bna, bwin, w["sink_rows"], w["wo"])
            xs[gi] = lax.cond(w["bound_ok"][l],
                              lambda *a: _attn_call(l, False, *a[:8], args[8], *a[8:]),
                              lambda *a: _attn_call(l, True, *a[:8], args[8], *a[8:]),
                              *args[:8], *args[9:])
    return tuple(xs)


def kernel(x_prompt, x_sample, mem_prompt, mem_sample, norm_g, w_in, q_norm_g, k_norm_g, rpb, sink, mem_norm_g, w_mem_kv, w_out):
    return _forward((x_prompt, x_sample), (mem_prompt, mem_sample), norm_g, w_in, q_norm_g, k_norm_g, rpb, sink,
                    mem_norm_g, w_mem_kv, w_out)
```

```python
import functools

import numpy as np
import jax
import jax.numpy as jnp
from jax import lax
from jax.experimental import pallas as pl
from jax.experimental.pallas import tpu as pltpu

D_MODEL = 1024
HEAD_DIM = 64
N_HEADS_NA = 6
N_HEADS_WIN = 6
N_KV_WIN = 2
N_HEADS_MEM = 4
D_NA = N_HEADS_NA * HEAD_DIM
D_WIN = N_HEADS_WIN * HEAD_DIM
D_KV_WIN = N_KV_WIN * HEAD_DIM
D_MEM = N_HEADS_MEM * HEAD_DIM
D_MIX = D_NA + D_WIN + D_MEM
D_K = D_NA + D_KV_WIN
D_PROJ = D_MIX + 2 * D_K + D_MIX
N_MEM = 256
GRID_W = 64
NA_ROWS = 8
NA_COLS = 16
WINDOW = 128
RMS_EPS = 1e-6
NEG_INF = -1e30
LOG2E = 1.4426950408889634
MAX_SHIFT_SLACK = 64.0

TILE = 256
TILE_ROWS = TILE // GRID_W
WIN_G = N_HEADS_WIN // N_KV_WIN
WIN_LANES = N_KV_WIN * WIN_G * WINDOW
VMEM_LIMIT_BYTES = 56 * 1024 * 1024

_BF16 = jnp.bfloat16
_F32 = jnp.float32
_NT = (((1,), (1,)), ((), ()))
_ONES_ROWS = 16


def _rms_rows(x, gain):
    ms = jnp.mean(x * x, axis=-1, keepdims=True)
    return x * lax.rsqrt(ms + RMS_EPS) * gain


def _head_norm_cols(p, gain):
    ms = jnp.mean(p * p, axis=0, keepdims=True)
    return p * lax.rsqrt(ms + RMS_EPS) * gain


def _proj_kernel(has_out, has_in, *refs):
    refs = list(refs)
    x_ref = refs.pop(0)
    mix_ref, wo_ref = (refs.pop(0), refs.pop(0)) if has_out else (None, None)
    ng_ref, wt_ref, gq_ref, gk_ref = [refs.pop(0) for _ in range(4)] if has_in else [None] * 4
    xo_ref = refs.pop(0) if has_out else None
    q_out, k_out, v_out, g_out, xn_scr = refs if has_in else [None] * 5
    ta = x_ref.shape[0]
    n_tiles = ta // TILE
    chunk = 256
    x_src = xo_ref if has_out else x_ref

    def normalise(j):
        rows = slice(j * TILE, (j + 1) * TILE)
        xn_scr[:, rows] = _rms_rows(x_src[rows, :], ng_ref[...]).astype(_BF16).T

    def queries(c0, j, p):
        for hh in range(chunk // HEAD_DIM):
            r0 = c0 + hh * HEAD_DIM
            qh = _head_norm_cols(p[hh * HEAD_DIM:(hh + 1) * HEAD_DIM, :], gq_ref[r0:r0 + HEAD_DIM, :])
            q_out[j, r0:r0 + HEAD_DIM, :] = qh.astype(_BF16)

    def keys(c0, j, p):
        kn = jnp.concatenate(
            [_head_norm_cols(p[hh * HEAD_DIM:(hh + 1) * HEAD_DIM, :], gk_ref[c0 + hh * HEAD_DIM:c0 + (hh + 1) * HEAD_DIM, :])
             for hh in range(chunk // HEAD_DIM)], axis=0)
        k_out[j * TILE:(j + 1) * TILE, c0:c0 + chunk] = kn.T.astype(_BF16)

    def values(c0, j, p):
        v_out[j, c0:c0 + chunk, :] = p.astype(_BF16)

    def gates(c0, j, p):
        g_out[j, c0:c0 + chunk, :] = (p * jax.nn.sigmoid(p)).astype(_BF16)

    def residual(c0, j, y):
        rows = slice(j * TILE, (j + 1) * TILE)
        xo_ref[rows, c0:c0 + chunk] = x_ref[rows, c0:c0 + chunk] + y

    def in_items(j):
        sections = [(0, D_MIX, queries), (D_MIX, D_K, keys), (D_MIX + 2 * D_K, D_MIX, gates), (D_MIX + D_K, D_K, values)]
        return [(functools.partial(
                    lambda r0: jnp.dot(wt_ref[r0:r0 + chunk, :], xn_scr[:, j * TILE:(j + 1) * TILE],
                                       preferred_element_type=_F32), base + c0),
                 functools.partial(epilogue, c0, j))
                for base, rows, epilogue in sections for c0 in range(0, rows, chunk)]

    def out_items(j):
        return [(functools.partial(
                    lambda c0: jnp.dot(mix_ref[j * TILE:(j + 1) * TILE, :], wo_ref[:, c0:c0 + chunk],
                                       preferred_element_type=_F32), c0),
                 functools.partial(residual, c0, j))
                for c0 in range(0, D_MODEL, chunk)]

    order = []
    if has_out:
        order += out_items(0)
    if has_in:
        order.append(functools.partial(normalise, 0))
        for j in range(n_tiles):
            items = in_items(j)
            third = len(items) // 3
            order += items[:third]
            if j + 1 < n_tiles:
                if has_out:
                    order += out_items(j + 1)
                order += items[third:2 * third]
                order.append(functools.partial(normalise, j + 1))
                order += items[2 * third:]
            else:
                order += items[third:]
    else:
        for j in range(1, n_tiles):
            order += out_items(j)

    pending = None
    for entry in order:
        if callable(entry):
            if pending is not None:
                pending[0](pending[1])
                pending = None
            entry()
            continue
        matmul, epilogue = entry
        result = matmul()
        if pending is not None:
            pending[0](pending[1])
        pending = (epilogue, result)
    pending[0](pending[1])


def _proj_call(layer, x, w, mix=None):
    b, t, _ = x.shape
    has_out, has_in = mix is not None, layer is not None
    ta = next(c for c in (1024, 512, TILE) if t % c == 0)
    nt = t // TILE
    tpb = ta // TILE
    tokens = lambda cols: pl.BlockSpec((None, ta, cols), lambda i, j: (i, j, 0))
    tiles = lambda rows: pl.BlockSpec((None, tpb, rows, TILE), lambda i, j: (i, j, 0, 0))
    per_layer = lambda l, *shape: pl.BlockSpec((None,) + shape, lambda i, j: (l,) + (0,) * len(shape),
                                               pipeline_mode=pl.Buffered(1))
    operands, in_specs, out_specs, out_shape, scratch = [x], [tokens(D_MODEL)], [], [], []
    if has_out:
        prev = (w["wo"].shape[0] - 1) if layer is None else layer - 1
        operands += [mix, w["wo"]]
        in_specs += [tokens(D_MIX), per_layer(prev, D_MIX, D_MODEL)]
        out_specs.append(tokens(D_MODEL))
        out_shape.append(jax.ShapeDtypeStruct(x.shape, x.dtype))
    if has_in:
        operands += [w["ng"], w["wt"], w["gq"], w["gk"]]
        in_specs += [per_layer(layer, 1, D_MODEL), per_layer(layer, D_PROJ, D_MODEL), per_layer(layer, D_MIX, 1),
                     per_layer(layer, D_K, 1)]
        out_specs += [tiles(D_MIX), tokens(D_K), tiles(D_K), tiles(D_MIX)]
        out_shape += [
            jax.ShapeDtypeStruct((b, nt, D_MIX, TILE), _BF16),
            jax.ShapeDtypeStruct((b, t, D_K), _BF16),
            jax.ShapeDtypeStruct((b, nt, D_K, TILE), _BF16),
            jax.ShapeDtypeStruct((b, nt, D_MIX, TILE), _BF16),
        ]
        scratch.append(pltpu.VMEM((D_MODEL, ta), _BF16))
    return pl.pallas_call(
        functools.partial(_proj_kernel, has_out, has_in),
        grid=(b, t // ta),
        in_specs=in_specs, out_specs=out_specs, out_shape=out_shape, scratch_shapes=scratch,
        compiler_params=pltpu.CompilerParams(
            dimension_semantics=("arbitrary", "arbitrary"), vmem_limit_bytes=VMEM_LIMIT_BYTES),
        name="proj",
    )(*operands)


def _memkv_kernel(mem_ref, g_ref, wt_ref, gk_ref, km_out, vm_out):
    xn = _rms_rows(mem_ref[...], g_ref[...]).astype(_BF16)
    p = lax.dot_general(wt_ref[...], xn, _NT, preferred_element_type=_F32)
    kn = jnp.concatenate(
        [_head_norm_cols(p[h * HEAD_DIM:(h + 1) * HEAD_DIM, :], gk_ref[h * HEAD_DIM:(h + 1) * HEAD_DIM, :])
         for h in range(N_HEADS_MEM)], axis=0)
    km_out[...] = kn.T.astype(_BF16)
    vm_out[...] = p[D_MEM:, :].astype(_BF16)


def _memkv_call(mem, g, wt, gk):
    bm = mem.shape[0]
    depth = g.shape[0]
    return pl.pallas_call(
        _memkv_kernel,
        grid=(depth, bm),
        in_specs=[
            pl.BlockSpec((None, N_MEM, D_MODEL), lambda l, i: (i, 0, 0)),
            pl.BlockSpec((None, 1, D_MODEL), lambda l, i: (l, 0, 0)),
            pl.BlockSpec((None, 2 * D_MEM, D_MODEL), lambda l, i: (l, 0, 0)),
            pl.BlockSpec((None, D_MEM, 1), lambda l, i: (l, 0, 0)),
        ],
        out_specs=[
            pl.BlockSpec((None, None, N_MEM, D_MEM), lambda l, i: (l, i, 0, 0)),
            pl.BlockSpec((None, None, D_MEM, N_MEM), lambda l, i: (l, i, 0, 0)),
        ],
        out_shape=[
            jax.ShapeDtypeStruct((depth, bm, N_MEM, D_MEM), _BF16),
            jax.ShapeDtypeStruct((depth, bm, D_MEM, N_MEM), _BF16),
        ],
        compiler_params=pltpu.CompilerParams(
            dimension_semantics=("arbitrary", "arbitrary"), vmem_limit_bytes=VMEM_LIMIT_BYTES),
        name="memkv",
    )(mem, g, wt, gk)


def _pad_head(q, odd):
    z = jnp.zeros_like(q)
    return jnp.concatenate([z, q] if odd else [q, z], axis=0)


class _Unit:
    def __init__(self, k_parts, rhs, bias_of, v_groups, sink, done):
        self.k_parts, self.rhs, self.bias_of = k_parts, rhs, bias_of
        self.v_groups, self.sink, self.done = v_groups, sink, done
        self.rows = [0]
        self._rhs = self.l = None
        self.accs = [None] * len(v_groups)

    def scores(self, j):
        if self._rhs is None:
            self._rhs = self.rhs()
        k_blk = self.k_parts[j]()
        r0, r = self.rows[j], k_blk.shape[0]
        if len(self.rows) == j + 1:
            self.rows.append(r0 + r)
        return jnp.dot(k_blk, self._rhs, preferred_element_type=_F32) + self.bias_of(r0, r)

    def probs(self, s):
        return jnp.exp2(s).astype(_BF16)

    def weigh(self, j, pb):
        for gi, (ls, vs) in enumerate(self.v_groups):
            v = vs[j]()
            v1 = jnp.concatenate([v, jnp.ones((_ONES_ROWS, v.shape[1]), v.dtype)], axis=0)
            d = jnp.dot(v1, pb[:, ls], preferred_element_type=_F32)
            self.accs[gi] = d if self.accs[gi] is None else self.accs[gi] + d

    def finish(self, sink_term=None):
        if self.sink is not None and sink_term is None:
            sink_term = self.sink()[1:2, :]
        outs = []
        for acc, (ls, _) in zip(self.accs, self.v_groups):
            l = acc[HEAD_DIM:HEAD_DIM + 1, :]
            if sink_term is not None:
                l = l + sink_term[:, ls]
            outs.append(acc[:HEAD_DIM, :] * (1.0 / l))
        self.done(outs)

    def run_exact(self, scr):
        m = None
        for j in range(len(self.k_parts)):
            s = self.scores(j)
            scr[self.rows[j]:self.rows[j + 1], :] = s
            mj = jnp.max(s, axis=0, keepdims=True)
            m = mj if m is None else jnp.maximum(m, mj)
        sink = None if self.sink is None else self.sink()[0:1, :]
        if sink is not None:
            m = jnp.maximum(m, sink)
        for j in range(len(self.k_parts)):
            self.weigh(j, self.probs(scr[self.rows[j]:self.rows[j + 1], :] - m))
        self.finish(None if sink is None else jnp.exp2(sink - m))


def _attn_kernel(exact_max, q_ref, g_ref, kp_ref, kc_ref, kn_ref, vp_ref, vc_ref, vn_ref, km_ref, vm_ref,
                 bna_ref, bwin_ref, sink_ref, bmem_ref, o_ref, mix_scr, *score_scr):
    n = pl.program_id(0)
    n_tiles = pl.num_programs(0)
    k_refs = (kp_ref, kc_ref, kn_ref)
    v_refs = (vp_ref, vc_ref, vn_ref)
    wb = WINDOW
    full = slice(0, TILE)
    scr_na, scr_win = score_scr if exact_max else (None, None)

    def gate_store(r0, lanes, o):
        gate = g_ref[r0:r0 + HEAD_DIM, lanes].astype(_F32)
        mix_scr[r0:r0 + HEAD_DIM, lanes] = (o * gate).astype(_BF16)

    def rows_of(ref, rows, lanes):
        return lambda: ref[rows, lanes]

    def na_unit(h):
        hrows = slice(h * HEAD_DIM, (h + 1) * HEAD_DIM)
        lanes = slice(128 * (h // 2), 128 * (h // 2) + 128)
        return _Unit(
            [rows_of(k_ref, full, lanes) for k_ref in k_refs],
            lambda: _pad_head(q_ref[hrows, :], h % 2),
            lambda r0, r: bna_ref[h, r0:r0 + r, :],
            [(full, [rows_of(v_ref, hrows, full) for v_ref in v_refs])], None,
            lambda outs: gate_store(h * HEAD_DIM, full, outs[0]))

    def win_unit(blk):
        qlanes = slice(blk * wb, (blk + 1) * wb)
        if blk == 0:
            variant = jnp.where(n == 0, 0, 1)
            src = [(0, wb), (1, 0), (1, wb)]
        else:
            variant = jnp.where(n == n_tiles - 1, 2, 1)
            src = [(1, 0), (1, wb), (2, 0)]

        def rhs():
            rows = []
            for kv in range(N_KV_WIN):
                q_cat = jnp.concatenate(
                    [q_ref[D_NA + (kv * WIN_G + g) * HEAD_DIM:D_NA + (kv * WIN_G + g + 1) * HEAD_DIM, qlanes]
                     for g in range(WIN_G)], axis=1)
                z = jnp.zeros_like(q_cat)
                rows.append(jnp.concatenate([z, q_cat] if kv else [q_cat, z], axis=1))
            return jnp.concatenate(rows, axis=0)

        def done(outs):
            for kv, o in enumerate(outs):
                for g in range(WIN_G):
                    gate_store(D_NA + (kv * WIN_G + g) * HEAD_DIM, qlanes, o[:, g * wb:(g + 1) * wb])

        groups = []
        for kv in range(N_KV_WIN):
            vrows = slice(D_NA + kv * HEAD_DIM, D_NA + (kv + 1) * HEAD_DIM)
            vs = [rows_of(v_refs[t], vrows, slice(r0, r0 + wb)) for t, r0 in src]
            groups.append((slice(kv * WIN_G * wb, (kv + 1) * WIN_G * wb), vs))
        return _Unit(
            [rows_of(k_refs[t], slice(r0, r0 + wb), slice(D_NA, D_NA + D_KV_WIN)) for t, r0 in src],
            rhs, lambda r0, r: bwin_ref[variant, r0:r0 + r, :], groups, lambda: sink_ref[...], done)

    def mem_unit(h):
        r0 = D_NA + D_WIN + h * HEAD_DIM
        lanes = slice(128 * (h // 2), 128 * (h // 2) + 128)
        return _Unit(
            [rows_of(km_ref, slice(0, N_MEM), lanes)],
            lambda: _pad_head(q_ref[r0:r0 + HEAD_DIM, :], h % 2),
            lambda r0_, r: bmem_ref[h],
            [(full, [rows_of(vm_ref, slice(h * HEAD_DIM, (h + 1) * HEAD_DIM), slice(0, N_MEM))])], None,
            lambda outs: gate_store(r0, full, outs[0]))

    def emit_chunk(c):
        cs = slice(c * 256, (c + 1) * 256)
        o_ref[:, cs] = mix_scr[cs, :].T

    units = ([mem_unit(h) for h in range(N_HEADS_MEM)] + [win_unit(b) for b in range(TILE // wb)]
             + [na_unit(h) for h in range(N_HEADS_NA)])
    chunks_ready_after = {3: [3], 5: [2], 9: [0], 11: [1]}
    pending = []

    def after_unit(ui):
        for c in chunks_ready_after.get(ui, []):
            pending.append(functools.partial(emit_chunk, c))

    if exact_max:
        for ui, unit in enumerate(units):
            unit.run_exact(scr_win if unit.sink is not None else scr_na)
            after_unit(ui)
            while pending:
                pending.pop(0)()
        return

    blocks = [(ui, j) for ui, unit in enumerate(units) for j in range(len(unit.k_parts))]
    s_q, p_q = {}, {}
    for t in range(len(blocks) + 2):
        if t < len(blocks):
            ui, j = blocks[t]
            s_q[t] = units[ui].scores(j)
        if pending:
            pending.pop(0)()
        if 0 <= t - 1 < len(blocks):
            ui, j = blocks[t - 1]
            p_q[t - 1] = units[ui].probs(s_q.pop(t - 1))
        if 0 <= t - 2 < len(blocks):
            ui, j = blocks[t - 2]
            units[ui].weigh(j, p_q.pop(t - 2))
            if j == len(units[ui].k_parts) - 1:
                units[ui].finish()
                after_unit(ui)
    while pending:
        pending.pop(0)()


def _attn_call(layer, exact_max, q, g, k, v, km, vm, mem_lo, bna, bwin, sink_rows, bmem):
    b, t, _ = k.shape
    nt = t // TILE
    last = nt - 1
    tile4 = lambda rows: (None, None, rows, TILE)
    scratch = [pltpu.VMEM((D_MIX, TILE), _BF16)]
    if exact_max:
        scratch += [pltpu.VMEM((3 * TILE, TILE), _F32), pltpu.VMEM((3 * WINDOW, WIN_LANES), _F32)]
    return pl.pallas_call(
        functools.partial(_attn_kernel, exact_max),
        grid=(nt, b),
        in_specs=[
            pl.BlockSpec(tile4(D_MIX), lambda n, i: (i, n, 0, 0)),
            pl.BlockSpec(tile4(D_MIX), lambda n, i: (i, n, 0, 0)),
            pl.BlockSpec((None, TILE, D_K), lambda n, i: (i, jnp.maximum(n - 1, 0), 0)),
            pl.BlockSpec((None, TILE, D_K), lambda n, i: (i, n, 0)),
            pl.BlockSpec((None, TILE, D_K), lambda n, i: (i, jnp.minimum(n + 1, last), 0)),
            pl.BlockSpec(tile4(D_K), lambda n, i: (i, jnp.maximum(n - 1, 0), 0, 0)),
            pl.BlockSpec(tile4(D_K), lambda n, i: (i, n, 0, 0)),
            pl.BlockSpec(tile4(D_K), lambda n, i: (i, jnp.minimum(n + 1, last), 0, 0)),
            pl.BlockSpec((None, None, N_MEM, D_MEM), lambda n, i: (layer, mem_lo + i, 0, 0)),
            pl.BlockSpec((None, None, D_MEM, N_MEM), lambda n, i: (layer, mem_lo + i, 0, 0)),
            pl.BlockSpec((None, None, N_HEADS_NA, 3 * TILE, TILE),
                         lambda n, i: (layer, jnp.where(n == 0, 0, jnp.where(n == last, 2, 1)), 0, 0, 0)),
            pl.BlockSpec((None, 3, 3 * WINDOW, WIN_LANES), lambda n, i: (layer, 0, 0, 0)),
            pl.BlockSpec((None, 2, WIN_LANES), lambda n, i: (layer, 0, 0)),
            pl.BlockSpec((None, N_HEADS_MEM, 1, TILE), lambda n, i: (layer, 0, 0, 0)),
        ],
        out_specs=pl.BlockSpec((None, TILE, D_MIX), lambda n, i: (i, n, 0)),
        out_shape=jax.ShapeDtypeStruct((b, t, D_MIX), _BF16),
        scratch_shapes=scratch,
        compiler_params=pltpu.CompilerParams(
            dimension_semantics=("arbitrary", "arbitrary"), vmem_limit_bytes=VMEM_LIMIT_BYTES),
        name="attn_exact" if exact_max else "attn",
    )(q, g, k, k, k, v, v, v, km, vm, bna, bwin, sink_rows, bmem)


def _na_index_tables():
    n_tiles = 4
    rows = n_tiles * TILE_ROWS
    rel_rows, row_ok = [], []
    for n in (0, 1, n_tiles - 1):
        qr = n * TILE_ROWS + np.arange(TILE_ROWS)
        kr = (n - 1) * TILE_ROWS + np.arange(3 * TILE_ROWS)
        r0 = np.clip(qr - NA_ROWS // 2, 0, rows - NA_ROWS)
        ok = ((kr[:, None] >= r0[None, :]) & (kr[:, None] < r0[None, :] + NA_ROWS)
              & (kr[:, None] >= 0) & (kr[:, None] < rows))
        rel_rows.append(np.clip(kr[:, None] - qr[None, :] + NA_ROWS - 1, 0, 2 * NA_ROWS - 2))
        row_ok.append(ok)
    qc = np.arange(GRID_W)
    kc = np.arange(GRID_W)
    cs = np.clip(qc - NA_COLS // 2, 0, GRID_W - NA_COLS)
    col_ok = (kc[:, None] >= cs[None, :]) & (kc[:, None] < cs[None, :] + NA_COLS)
    rel_col = np.clip(kc[:, None] - qc[None, :] + NA_COLS - 1, 0, 2 * NA_COLS - 2)
    rel_rows = np.stack(rel_rows)
    row_ok = np.stack(row_ok)
    valid = row_ok[:, :, None, :, None] & col_ok[None, None, :, None, :]
    return rel_rows.astype(np.int32), rel_col.astype(np.int32), valid.reshape(3, 3 * TILE, TILE)


def _na_bias(rpb2, shift):
    depth = rpb2.shape[0]
    rel_rows, rel_col, valid = _na_index_tables()
    t1 = jnp.take(rpb2 - shift[:, :, None, None], jnp.asarray(rel_col.reshape(-1)), axis=3)
    t1 = t1.reshape(depth, N_HEADS_NA, 2 * NA_ROWS - 1, GRID_W, GRID_W)
    t2 = jnp.take(t1, jnp.asarray(rel_rows.reshape(-1)), axis=2)
    t2 = t2.reshape(depth, N_HEADS_NA, 3, 3 * TILE_ROWS, TILE_ROWS, GRID_W, GRID_W)
    t2 = jnp.transpose(t2, (0, 2, 1, 3, 5, 4, 6)).reshape(depth, 3, N_HEADS_NA, 3 * TILE, TILE)
    return jnp.where(jnp.asarray(valid)[None, :, None], t2, NEG_INF).astype(_F32)


def _win_bias(shift):
    i = np.arange(WINDOW)
    w = np.arange(3 * WINDOW)
    dist = (i[None, :] - w[:, None] + WINDOW).astype(np.float32)
    band = np.abs(dist) <= WINDOW
    edge = np.stack([w >= WINDOW, w >= 0, w < 2 * WINDOW])
    valid = band[None] & edge[:, :, None]
    slopes = 2.0 ** (-8.0 * jnp.arange(1, N_HEADS_WIN + 1, dtype=_F32) / N_HEADS_WIN)
    ali = -slopes[None, :, None] * jnp.abs(jnp.asarray(dist))[:, None, :] * LOG2E
    shifted = ali[None] - shift[:, None, :, None]
    bias = jnp.where(jnp.asarray(valid)[None, :, :, None, :], shifted[:, None], NEG_INF)
    return bias.reshape(shift.shape[0], 3, 3 * WINDOW, WIN_LANES).astype(_F32)


def _prep_weights(norm_g, w_in, q_norm_g, k_norm_g, rpb, sink, mem_norm_g, w_mem_kv, w_out):
    depth = w_in.shape[0]
    sizes = [D_NA, D_NA, D_NA, D_NA, D_WIN, D_KV_WIN, D_KV_WIN, D_WIN, D_MEM, D_MEM]
    offs = np.concatenate([[0], np.cumsum(sizes)])
    seg = lambda i: w_in[:, :, offs[i]:offs[i + 1]]
    na_q, na_k, na_v, na_g, wq, wk, wv, wg, mq, mg = [seg(i) for i in range(10)]
    w_perm = jnp.concatenate([na_q, wq, mq, na_k, wk, na_v, wv, na_g, wg, mg], axis=-1)
    wt = jnp.swapaxes(w_perm, 1, 2).astype(_BF16)
    q_scale = HEAD_DIM ** -0.5 * LOG2E
    heads = (N_HEADS_NA, N_HEADS_WIN, N_HEADS_MEM)
    gq = jnp.concatenate([jnp.tile(q_norm_g[:, i], (1, n)) for i, n in enumerate(heads)], axis=-1) * q_scale
    gk = jnp.concatenate([jnp.tile(k_norm_g[:, 0], (1, N_HEADS_NA)), jnp.tile(k_norm_g[:, 1], (1, N_KV_WIN))], axis=-1)
    gkm = jnp.tile(k_norm_g[:, 2], (1, N_HEADS_MEM))
    sink2 = sink.astype(_F32) * LOG2E
    rpb2 = rpb.astype(_F32) * LOG2E

    q_bound = HEAD_DIM ** 0.5 * jnp.max(jnp.abs(q_norm_g), axis=-1) * q_scale
    k_bound = HEAD_DIM ** 0.5 * jnp.max(jnp.abs(k_norm_g), axis=-1)
    qk = q_bound * k_bound
    rpb_max, rpb_min = jnp.max(rpb2, axis=(2, 3)), jnp.min(rpb2, axis=(2, 3))
    shift_na = qk[:, 0:1] + rpb_max
    shift_win = jnp.maximum(qk[:, 1:2], sink2)
    shift_mem = jnp.broadcast_to(qk[:, 2:3], (depth, N_HEADS_MEM))
    slack = jnp.max(2.0 * qk + jnp.stack([jnp.max(rpb_max - rpb_min, axis=1), jnp.zeros((depth,), _F32),
                                          jnp.zeros((depth,), _F32)], axis=1), axis=1)
    bound_ok = slack <= MAX_SHIFT_SLACK
    use = bound_ok[:, None].astype(_F32)
    shift_na, shift_win, shift_mem = shift_na * use, shift_win * use, shift_mem * use
    sink_logit = jnp.repeat(sink2 - shift_win, WINDOW, axis=-1)
    return dict(
        ng=norm_g.reshape(depth, 1, D_MODEL), wt=wt,
        gq=gq.reshape(depth, D_MIX, 1), gk=gk.reshape(depth, D_K, 1), gkm=gkm.reshape(depth, D_MEM, 1),
        bna=_na_bias(rpb2, shift_na), bwin=_win_bias(shift_win),
        sink_rows=jnp.stack([sink_logit, jnp.exp2(sink_logit)], axis=1),
        bmem=jnp.broadcast_to(-shift_mem[:, :, None, None], (depth, N_HEADS_MEM, 1, TILE)),
        mg=mem_norm_g.reshape(depth, 1, D_MODEL),
        wmt=jnp.swapaxes(w_mem_kv, 1, 2).astype(_BF16),
        wo=w_out.astype(_BF16),
        bound_ok=bound_ok,
    )


def _forward(xs, mems, norm_g, w_in, q_norm_g, k_norm_g, rpb, sink, mem_norm_g, w_mem_kv, w_out):
    depth = w_in.shape[0]
    w = _prep_weights(norm_g, w_in, q_norm_g, k_norm_g, rpb, sink, mem_norm_g, w_mem_kv, w_out)
    km, vm = _memkv_call(jnp.concatenate(mems, axis=0), w["mg"], w["wmt"], w["gkm"])
    mem_off = np.concatenate([[0], np.cumsum([m.shape[0] for m in mems])])
    xs = list(xs)
    mixes = [None] * len(xs)
    tables = (w["bna"], w["bwin"], w["sink_rows"], w["bmem"])
    for l in range(depth):
        for gi in range(len(xs)):
            outs = _proj_call(l, xs[gi], w, mixes[gi])
            if mixes[gi] is not None:
                xs[gi], outs = outs[0], outs[1:]
            q, k, v, g = outs
            lo = int(mem_off[gi])
            mixes[gi] = lax.cond(w["bound_ok"][l],
                                 lambda *a: _attn_call(l, False, *a[:6], lo, *a[6:]),
                                 lambda *a: _attn_call(l, True, *a[:6], lo, *a[6:]),
                                 q, g, k, v, km, vm, *tables)
    return tuple(_proj_call(None, x, w, mix)[0] for x, mix in zip(xs, mixes))


def kernel(x_prompt, x_sample, mem_prompt, mem_sample, norm_g, w_in, q_norm_g, k_norm_g, rpb, sink, mem_norm_g, w_mem_kv, w_out):
    return _forward((x_prompt, x_sample), (mem_prompt, mem_sample), norm_g, w_in, q_norm_g, k_norm_g, rpb, sink,
                    mem_norm_g, w_mem_kv, w_out)
```

```python
import functools

import numpy as np
import jax
import jax.numpy as jnp
from jax import lax
from jax.experimental import pallas as pl
from jax.experimental.pallas import tpu as pltpu

D_MODEL = 1024
HEAD_DIM = 64
N_HEADS_NA = 6
N_HEADS_WIN = 6
N_KV_WIN = 2
N_HEADS_MEM = 4
D_NA = N_HEADS_NA * HEAD_DIM
D_WIN = N_HEADS_WIN * HEAD_DIM
D_KV_WIN = N_KV_WIN * HEAD_DIM
D_MEM = N_HEADS_MEM * HEAD_DIM
D_MIX = D_NA + D_WIN + D_MEM
D_K = D_NA + D_KV_WIN
D_PROJ = D_MIX + 2 * D_K + D_MIX
N_MEM = 256
GRID_W = 64
NA_ROWS = 8
NA_COLS = 16
WINDOW = 128
RMS_EPS = 1e-6
NEG_INF = -1e30
LOG2E = 1.4426950408889634
MAX_SHIFT_SLACK = 64.0

TILE = 256
TILE_ROWS = TILE // GRID_W
TILES_PER_STEP = 4
WIN_G = N_HEADS_WIN // N_KV_WIN
WIN_LANES = N_KV_WIN * WIN_G * WINDOW
VMEM_LIMIT_BYTES = 56 * 1024 * 1024

_BF16 = jnp.bfloat16
_F32 = jnp.float32
_NT = (((1,), (1,)), ((), ()))
_ONES_ROWS = 16


def _rms_rows(x, gain):
    ms = jnp.mean(x * x, axis=-1, keepdims=True)
    return x * lax.rsqrt(ms + RMS_EPS) * gain


def _head_norm_cols(p, gain):
    ms = jnp.mean(p * p, axis=0, keepdims=True)
    return p * lax.rsqrt(ms + RMS_EPS) * gain


def _proj_kernel(has_out, has_in, *refs):
    refs = list(refs)
    x_ref = refs.pop(0)
    mix_ref, wo_ref = (refs.pop(0), refs.pop(0)) if has_out else (None, None)
    ng_ref, wt_ref, gq_ref, gk_ref = [refs.pop(0) for _ in range(4)] if has_in else [None] * 4
    xo_ref = refs.pop(0) if has_out else None
    q_out, k_out, v_out, g_out, xn_scr = refs if has_in else [None] * 5
    ta = x_ref.shape[0]
    n_tiles = ta // TILE
    chunk = 256
    x_src = xo_ref if has_out else x_ref

    def normalise(j):
        rows = slice(j * TILE, (j + 1) * TILE)
        xn_scr[:, rows] = _rms_rows(x_src[rows, :], ng_ref[...]).astype(_BF16).T

    def queries(c0, j, p):
        for hh in range(chunk // HEAD_DIM):
            r0 = c0 + hh * HEAD_DIM
            qh = _head_norm_cols(p[hh * HEAD_DIM:(hh + 1) * HEAD_DIM, :], gq_ref[r0:r0 + HEAD_DIM, :])
            q_out[j, r0:r0 + HEAD_DIM, :] = qh.astype(_BF16)

    def keys(c0, j, p):
        kn = jnp.concatenate(
            [_head_norm_cols(p[hh * HEAD_DIM:(hh + 1) * HEAD_DIM, :], gk_ref[c0 + hh * HEAD_DIM:c0 + (hh + 1) * HEAD_DIM, :])
             for hh in range(chunk // HEAD_DIM)], axis=0)
        k_out[j * TILE:(j + 1) * TILE, c0:c0 + chunk] = kn.T.astype(_BF16)

    def values(c0, j, p):
        v_out[j, c0:c0 + chunk, :] = p.astype(_BF16)

    def gates(c0, j, p):
        g_out[j, c0:c0 + chunk, :] = (p * jax.nn.sigmoid(p)).astype(_BF16)

    def residual(c0, j, y):
        rows = slice(j * TILE, (j + 1) * TILE)
        xo_ref[rows, c0:c0 + chunk] = x_ref[rows, c0:c0 + chunk] + y

    def in_items(j):
        sections = [(0, D_MIX, queries), (D_MIX, D_K, keys), (D_MIX + 2 * D_K, D_MIX, gates), (D_MIX + D_K, D_K, values)]
        return [(functools.partial(
                    lambda r0: jnp.dot(wt_ref[r0:r0 + chunk, :], xn_scr[:, j * TILE:(j + 1) * TILE],
                                       preferred_element_type=_F32), base + c0),
                 functools.partial(epilogue, c0, j))
                for base, rows, epilogue in sections for c0 in range(0, rows, chunk)]

    def out_items(j):
        return [(functools.partial(
                    lambda c0: jnp.dot(mix_ref[j * TILE:(j + 1) * TILE, :], wo_ref[:, c0:c0 + chunk],
                                       preferred_element_type=_F32), c0),
                 functools.partial(residual, c0, j))
                for c0 in range(0, D_MODEL, chunk)]

    order = []
    if has_out:
        order += out_items(0)
    if has_in:
        order.append(functools.partial(normalise, 0))
        for j in range(n_tiles):
            items = in_items(j)
            third = len(items) // 3
            order += items[:third]
            if j + 1 < n_tiles:
                if has_out:
                    order += out_items(j + 1)
                order += items[third:2 * third]
                order.append(functools.partial(normalise, j + 1))
                order += items[2 * third:]
            else:
                order += items[third:]
    else:
        for j in range(1, n_tiles):
            order += out_items(j)

    pending = None
    for entry in order:
        if callable(entry):
            if pending is not None:
                pending[0](pending[1])
                pending = None
            entry()
            continue
        matmul, epilogue = entry
        result = matmul()
        if pending is not None:
            pending[0](pending[1])
        pending = (epilogue, result)
    pending[0](pending[1])


def _proj_call(layer, x, w, mix=None):
    b, t, _ = x.shape
    has_out, has_in = mix is not None, layer is not None
    ta = next(c for c in (1024, 512, TILE) if t % c == 0)
    nt = t // TILE
    tpb = ta // TILE
    tokens = lambda cols: pl.BlockSpec((None, ta, cols), lambda i, j: (i, j, 0))
    tiles = lambda rows: pl.BlockSpec((None, tpb, rows, TILE), lambda i, j: (i, j, 0, 0))
    per_layer = lambda l, *shape: pl.BlockSpec((None,) + shape, lambda i, j: (l,) + (0,) * len(shape),
                                               pipeline_mode=pl.Buffered(1))
    operands, in_specs, out_specs, out_shape, scratch = [x], [tokens(D_MODEL)], [], [], []
    if has_out:
        prev = (w["wo"].shape[0] - 1) if layer is None else layer - 1
        operands += [mix, w["wo"]]
        in_specs += [tokens(D_MIX), per_layer(prev, D_MIX, D_MODEL)]
        out_specs.append(tokens(D_MODEL))
        out_shape.append(jax.ShapeDtypeStruct(x.shape, x.dtype))
    if has_in:
        operands += [w["ng"], w["wt"], w["gq"], w["gk"]]
        in_specs += [per_layer(layer, 1, D_MODEL), per_layer(layer, D_PROJ, D_MODEL), per_layer(layer, D_MIX, 1),
                     per_layer(layer, D_K, 1)]
        out_specs += [tiles(D_MIX), tokens(D_K), tiles(D_K), tiles(D_MIX)]
        out_shape += [
            jax.ShapeDtypeStruct((b, nt, D_MIX, TILE), _BF16),
            jax.ShapeDtypeStruct((b, t, D_K), _BF16),
            jax.ShapeDtypeStruct((b, nt, D_K, TILE), _BF16),
            jax.ShapeDtypeStruct((b, nt, D_MIX, TILE), _BF16),
        ]
        scratch.append(pltpu.VMEM((D_MODEL, ta), _BF16))
    return pl.pallas_call(
        functools.partial(_proj_kernel, has_out, has_in),
        grid=(b, t // ta),
        in_specs=in_specs, out_specs=out_specs, out_shape=out_shape, scratch_shapes=scratch,
        compiler_params=pltpu.CompilerParams(
            dimension_semantics=("arbitrary", "arbitrary"), vmem_limit_bytes=VMEM_LIMIT_BYTES),
        name="proj",
    )(*operands)


def _memkv_kernel(mem_ref, g_ref, wt_ref, gk_ref, km_out, vm_out):
    xn = _rms_rows(mem_ref[...], g_ref[...]).astype(_BF16)
    p = lax.dot_general(wt_ref[...], xn, _NT, preferred_element_type=_F32)
    kn = jnp.concatenate(
        [_head_norm_cols(p[h * HEAD_DIM:(h + 1) * HEAD_DIM, :], gk_ref[h * HEAD_DIM:(h + 1) * HEAD_DIM, :])
         for h in range(N_HEADS_MEM)], axis=0)
    km_out[...] = kn.T.astype(_BF16)
    vm_out[...] = p[D_MEM:, :].astype(_BF16)


def _memkv_call(mem, g, wt, gk):
    bm = mem.shape[0]
    depth = g.shape[0]
    return pl.pallas_call(
        _memkv_kernel,
        grid=(depth, bm),
        in_specs=[
            pl.BlockSpec((None, N_MEM, D_MODEL), lambda l, i: (i, 0, 0)),
            pl.BlockSpec((None, 1, D_MODEL), lambda l, i: (l, 0, 0)),
            pl.BlockSpec((None, 2 * D_MEM, D_MODEL), lambda l, i: (l, 0, 0)),
            pl.BlockSpec((None, D_MEM, 1), lambda l, i: (l, 0, 0)),
        ],
        out_specs=[
            pl.BlockSpec((None, None, N_MEM, D_MEM), lambda l, i: (l, i, 0, 0)),
            pl.BlockSpec((None, None, D_MEM, N_MEM), lambda l, i: (l, i, 0, 0)),
        ],
        out_shape=[
            jax.ShapeDtypeStruct((depth, bm, N_MEM, D_MEM), _BF16),
            jax.ShapeDtypeStruct((depth, bm, D_MEM, N_MEM), _BF16),
        ],
        compiler_params=pltpu.CompilerParams(
            dimension_semantics=("arbitrary", "arbitrary"), vmem_limit_bytes=VMEM_LIMIT_BYTES),
        name="memkv",
    )(mem, g, wt, gk)


def _pad_head(q, odd):
    z = jnp.zeros_like(q)
    return jnp.concatenate([z, q] if odd else [q, z], axis=0)


class _Unit:
    def __init__(self, k_parts, rhs, bias_of, v_groups, sink, done):
        self.k_parts, self.rhs, self.bias_of = k_parts, rhs, bias_of
        self.v_groups, self.sink, self.done = v_groups, sink, done
        self.rows = [0]
        self._rhs = self.l = None
        self.accs = [None] * len(v_groups)

    def scores(self, j):
        if self._rhs is None:
            self._rhs = self.rhs()
        k_blk = self.k_parts[j]()
        r0, r = self.rows[j], k_blk.shape[0]
        if len(self.rows) == j + 1:
            self.rows.append(r0 + r)
        return jnp.dot(k_blk, self._rhs, preferred_element_type=_F32) + self.bias_of(r0, r)

    def probs(self, s):
        return jnp.exp2(s).astype(_BF16)

    def weigh(self, j, pb):
        for gi, (ls, vs) in enumerate(self.v_groups):
            v = vs[j]()
            v1 = jnp.concatenate([v, jnp.ones((_ONES_ROWS, v.shape[1]), v.dtype)], axis=0)
            d = jnp.dot(v1, pb[:, ls], preferred_element_type=_F32)
            self.accs[gi] = d if self.accs[gi] is None else self.accs[gi] + d

    def finish(self, sink_term=None):
        if self.sink is not None and sink_term is None:
            sink_term = self.sink()[1:2, :]
        outs = []
        for acc, (ls, _) in zip(self.accs, self.v_groups):
            l = acc[HEAD_DIM:HEAD_DIM + 1, :]
            if sink_term is not None:
                l = l + sink_term[:, ls]
            outs.append(acc[:HEAD_DIM, :] * (1.0 / l))
        self.done(outs)

    def run_exact(self, scr):
        m = None
        for j in range(len(self.k_parts)):
            s = self.scores(j)
            scr[self.rows[j]:self.rows[j + 1], :] = s
            mj = jnp.max(s, axis=0, keepdims=True)
            m = mj if m is None else jnp.maximum(m, mj)
        sink = None if self.sink is None else self.sink()[0:1, :]
        if sink is not None:
            m = jnp.maximum(m, sink)
        for j in range(len(self.k_parts)):
            self.weigh(j, self.probs(scr[self.rows[j]:self.rows[j + 1], :] - m))
        self.finish(None if sink is None else jnp.exp2(sink - m))


def _attn_kernel(exact_max, tps, q_ref, g_ref, kp_ref, kc_ref, kn_ref, vp_ref, vc_ref, vn_ref, km_ref, vm_ref,
                 bna_first_ref, bna_last_ref, bna_mid_ref, bwin_ref, sink_ref, bmem_ref, o_ref, mix_scr, *score_scr):
    n = pl.program_id(0)
    n_steps = pl.num_programs(0)
    wb = WINDOW
    full = slice(0, TILE)
    scr_na, scr_win = score_scr if exact_max else (None, None)

    def k_rows(tile, r0, r, lanes):
        if tile < 0:
            return lambda: kp_ref[r0:r0 + r, lanes]
        if tile >= tps:
            return lambda: kn_ref[r0:r0 + r, lanes]
        return lambda: kc_ref[tile * TILE + r0:tile * TILE + r0 + r, lanes]

    def v_cols(tile, rows, c0, c):
        if tile < 0:
            return lambda: vp_ref[rows, c0:c0 + c]
        if tile >= tps:
            return lambda: vn_ref[rows, c0:c0 + c]
        return lambda: vc_ref[tile, rows, c0:c0 + c]

    def gate_store(i, r0, lanes, o):
        gate = g_ref[i, r0:r0 + HEAD_DIM, lanes].astype(_F32)
        mix_scr[i, r0:r0 + HEAD_DIM, lanes] = (o * gate).astype(_BF16)

    def na_unit(i, h):
        hrows = slice(h * HEAD_DIM, (h + 1) * HEAD_DIM)
        lanes = slice(128 * (h // 2), 128 * (h // 2) + 128)
        bias_ref = bna_first_ref if i == 0 else bna_last_ref if i == tps - 1 else bna_mid_ref
        return _Unit(
            [k_rows(i + d, 0, TILE, lanes) for d in (-1, 0, 1)],
            lambda: _pad_head(q_ref[i, hrows, :], h % 2),
            lambda r0, r: bias_ref[h, r0:r0 + r, :],
            [(full, [v_cols(i + d, hrows, 0, TILE) for d in (-1, 0, 1)])], None,
            lambda outs: gate_store(i, h * HEAD_DIM, full, outs[0]))

    def win_unit(i, blk):
        qlanes = slice(blk * wb, (blk + 1) * wb)
        per_tile = TILE // wb
        src = [divmod(i * per_tile + blk + d, per_tile) for d in (-1, 0, 1)]
        src = [(t, b * wb) for t, b in src]
        if i == 0 and blk == 0:
            variant = jnp.where(n == 0, 0, 1)
        elif i == tps - 1 and blk == per_tile - 1:
            variant = jnp.where(n == n_steps - 1, 2, 1)
        else:
            variant = 1

        def rhs():
            rows = []
            for kv in range(N_KV_WIN):
                q_cat = jnp.concatenate(
                    [q_ref[i, D_NA + (kv * WIN_G + g) * HEAD_DIM:D_NA + (kv * WIN_G + g + 1) * HEAD_DIM, qlanes]
                     for g in range(WIN_G)], axis=1)
                z = jnp.zeros_like(q_cat)
                rows.append(jnp.concatenate([z, q_cat] if kv else [q_cat, z], axis=1))
            return jnp.concatenate(rows, axis=0)

        def done(outs):
            for kv, o in enumerate(outs):
                for g in range(WIN_G):
                    gate_store(i, D_NA + (kv * WIN_G + g) * HEAD_DIM, qlanes, o[:, g * wb:(g + 1) * wb])

        groups = []
        for kv in range(N_KV_WIN):
            vrows = slice(D_NA + kv * HEAD_DIM, D_NA + (kv + 1) * HEAD_DIM)
            groups.append((slice(kv * WIN_G * wb, (kv + 1) * WIN_G * wb), [v_cols(t, vrows, r0, wb) for t, r0 in src]))
        return _Unit(
            [k_rows(t, r0, wb, slice(D_NA, D_NA + D_KV_WIN)) for t, r0 in src],
            rhs, lambda r0, r: bwin_ref[variant, r0:r0 + r, :], groups, lambda: sink_ref[...], done)

    def mem_unit(i, h):
        r0 = D_NA + D_WIN + h * HEAD_DIM
        lanes = slice(128 * (h // 2), 128 * (h // 2) + 128)
        return _Unit(
            [lambda: km_ref[:, lanes]],
            lambda: _pad_head(q_ref[i, r0:r0 + HEAD_DIM, :], h % 2),
            lambda r0_, r: bmem_ref[h],
            [(full, [lambda: vm_ref[h * HEAD_DIM:(h + 1) * HEAD_DIM, :]])], None,
            lambda outs: gate_store(i, r0, full, outs[0]))

    def emit_chunk(i, c):
        cs = slice(c * 256, (c + 1) * 256)
        o_ref[i * TILE:(i + 1) * TILE, cs] = mix_scr[i, cs, :].T

    units, ready = [], {}
    for i in range(tps):
        base = len(units)
        units += ([mem_unit(i, h) for h in range(N_HEADS_MEM)] + [win_unit(i, b) for b in range(TILE // wb)]
                  + [na_unit(i, h) for h in range(N_HEADS_NA)])
        for after, chunks in {3: [3], 5: [2], 9: [0], 11: [1]}.items():
            ready[base + after] = [functools.partial(emit_chunk, i, c) for c in chunks]
    pending = []

    if exact_max:
        for ui, unit in enumerate(units):
            unit.run_exact(scr_win if unit.sink is not None else scr_na)
            for emit in ready.get(ui, []):
                emit()
        return

    blocks = [(ui, j) for ui, unit in enumerate(units) for j in range(len(unit.k_parts))]
    s_q, p_q = {}, {}
    for t in range(len(blocks) + 2):
        if t < len(blocks):
            ui, j = blocks[t]
            s_q[t] = units[ui].scores(j)
        if pending:
            pending.pop(0)()
        if 0 <= t - 1 < len(blocks):
            ui, j = blocks[t - 1]
            p_q[t - 1] = units[ui].probs(s_q.pop(t - 1))
        if 0 <= t - 2 < len(blocks):
            ui, j = blocks[t - 2]
            units[ui].weigh(j, p_q.pop(t - 2))
            if j == len(units[ui].k_parts) - 1:
                units[ui].finish()
                pending.extend(ready.get(ui, []))
    while pending:
        pending.pop(0)()


def _attn_call(layer, exact_max, q, g, k, v, km, vm, mem_lo, bna, bwin, sink_rows, bmem):
    b, t, _ = k.shape
    nt = t // TILE
    tps = next(c for c in (TILES_PER_STEP, 3, 2) if nt % c == 0)
    steps = nt // tps
    last = steps - 1
    step_tiles = lambda rows: pl.BlockSpec((None, tps, rows, TILE), lambda n, i: (i, n, 0, 0))
    one_tile = lambda rows, tile_of: pl.BlockSpec((None, None, rows, TILE), lambda n, i: (i, tile_of(n), 0, 0))
    prev_tile = lambda n: jnp.maximum(n * tps - 1, 0)
    next_tile = lambda n: jnp.minimum((n + 1) * tps, nt - 1)
    const = lambda shape, index: pl.BlockSpec(shape, index, pipeline_mode=pl.Buffered(1))
    na_table = lambda variant_of: const((None, None, N_HEADS_NA, 3 * TILE, TILE),
                                        lambda n, i: (layer, variant_of(n), 0, 0, 0))
    scratch = [pltpu.VMEM((tps, D_MIX, TILE), _BF16)]
    if exact_max:
        scratch += [pltpu.VMEM((3 * TILE, TILE), _F32), pltpu.VMEM((3 * WINDOW, WIN_LANES), _F32)]
    return pl.pallas_call(
        functools.partial(_attn_kernel, exact_max, tps),
        grid=(steps, b),
        in_specs=[
            step_tiles(D_MIX),
            step_tiles(D_MIX),
            pl.BlockSpec((None, TILE, D_K), lambda n, i: (i, prev_tile(n), 0)),
            pl.BlockSpec((None, tps * TILE, D_K), lambda n, i: (i, n, 0)),
            pl.BlockSpec((None, TILE, D_K), lambda n, i: (i, next_tile(n), 0)),
            one_tile(D_K, prev_tile),
            step_tiles(D_K),
            one_tile(D_K, next_tile),
            pl.BlockSpec((None, None, N_MEM, D_MEM), lambda n, i: (layer, mem_lo + i, 0, 0)),
            pl.BlockSpec((None, None, D_MEM, N_MEM), lambda n, i: (layer, mem_lo + i, 0, 0)),
            na_table(lambda n: jnp.where(n == 0, 0, 1)),
            na_table(lambda n: jnp.where(n == last, 2, 1)),
            na_table(lambda n: 1),
            const((None, 3, 3 * WINDOW, WIN_LANES), lambda n, i: (layer, 0, 0, 0)),
            pl.BlockSpec((None, 2, WIN_LANES), lambda n, i: (layer, 0, 0)),
            pl.BlockSpec((None, N_HEADS_MEM, 1, TILE), lambda n, i: (layer, 0, 0, 0)),
        ],
        out_specs=pl.BlockSpec((None, tps * TILE, D_MIX), lambda n, i: (i, n, 0)),
        out_shape=jax.ShapeDtypeStruct((b, t, D_MIX), _BF16),
        scratch_shapes=scratch,
        compiler_params=pltpu.CompilerParams(
            dimension_semantics=("arbitrary", "arbitrary"), vmem_limit_bytes=VMEM_LIMIT_BYTES),
        name="attn_exact" if exact_max else "attn",
    )(q, g, k, k, k, v, v, v, km, vm, bna, bna, bna, bwin, sink_rows, bmem)


def _na_index_tables():
    n_tiles = 4
    rows = n_tiles * TILE_ROWS
    rel_rows, row_ok = [], []
    for n in (0, 1, n_tiles - 1):
        qr = n * TILE_ROWS + np.arange(TILE_ROWS)
        kr = (n - 1) * TILE_ROWS + np.arange(3 * TILE_ROWS)
        r0 = np.clip(qr - NA_ROWS // 2, 0, rows - NA_ROWS)
        ok = ((kr[:, None] >= r0[None, :]) & (kr[:, None] < r0[None, :] + NA_ROWS)
              & (kr[:, None] >= 0) & (kr[:, None] < rows))
        rel_rows.append(np.clip(kr[:, None] - qr[None, :] + NA_ROWS - 1, 0, 2 * NA_ROWS - 2))
        row_ok.append(ok)
    qc = np.arange(GRID_W)
    kc = np.arange(GRID_W)
    cs = np.clip(qc - NA_COLS // 2, 0, GRID_W - NA_COLS)
    col_ok = (kc[:, None] >= cs[None, :]) & (kc[:, None] < cs[None, :] + NA_COLS)
    rel_col = np.clip(kc[:, None] - qc[None, :] + NA_COLS - 1, 0, 2 * NA_COLS - 2)
    return np.stack(rel_rows), np.stack(row_ok), rel_col, col_ok


def _na_table_kernel(rel_rows, row_ok, t1_ref, out_ref):
    left_half = lax.broadcasted_iota(jnp.int32, (GRID_W, 2 * GRID_W), 1) < GRID_W
    masked = jnp.full((GRID_W, 2 * GRID_W), NEG_INF, _F32)
    for v in range(3):
        for kr in range(3 * TILE_ROWS):
            for pair in range(TILE_ROWS // 2):
                halves = [t1_ref[int(rel_rows[v, kr, qr])] if row_ok[v, kr, qr] else masked
                          for qr in (2 * pair, 2 * pair + 1)]
                out_ref[v, kr * GRID_W:(kr + 1) * GRID_W, pair * 2 * GRID_W:(pair + 1) * 2 * GRID_W] = (
                    jnp.where(left_half, halves[0], halves[1]))


def _na_bias(rpb2, shift):
    depth = rpb2.shape[0]
    rel_rows, row_ok, rel_col, col_ok = _na_index_tables()
    onehot = (rel_col.reshape(-1)[None, :] == np.arange(2 * NA_COLS - 1)[:, None]).astype(np.float32)
    t1 = jnp.einsum("lhab,bn->lhan", rpb2 - shift[:, :, None, None], jnp.asarray(onehot),
                    precision=lax.Precision.HIGHEST)
    t1 = jnp.where(jnp.asarray(col_ok.reshape(-1)), t1, NEG_INF).reshape(depth, N_HEADS_NA, 2 * NA_ROWS - 1, GRID_W, GRID_W)
    t1 = jnp.concatenate([t1, t1], axis=-1)
    return pl.pallas_call(
        functools.partial(_na_table_kernel, rel_rows, row_ok),
        grid=(depth, N_HEADS_NA),
        in_specs=[pl.BlockSpec((None, None, 2 * NA_ROWS - 1, GRID_W, 2 * GRID_W), lambda l, h: (l, h, 0, 0, 0))],
        out_specs=pl.BlockSpec((None, 3, None, 3 * TILE, TILE), lambda l, h: (l, 0, h, 0, 0)),
        out_shape=jax.ShapeDtypeStruct((depth, 3, N_HEADS_NA, 3 * TILE, TILE), _F32),
        compiler_params=pltpu.CompilerParams(
            dimension_semantics=("arbitrary", "arbitrary"), vmem_limit_bytes=VMEM_LIMIT_BYTES),
        name="na_table",
    )(t1)


def _win_bias(shift):
    i = np.arange(WINDOW)
    w = np.arange(3 * WINDOW)
    dist = (i[None, :] - w[:, None] + WINDOW).astype(np.float32)
    band = np.abs(dist) <= WINDOW
    edge = np.stack([w >= WINDOW, w >= 0, w < 2 * WINDOW])
    valid = band[None] & edge[:, :, None]
    slopes = 2.0 ** (-8.0 * jnp.arange(1, N_HEADS_WIN + 1, dtype=_F32) / N_HEADS_WIN)
    ali = -slopes[None, :, None] * jnp.abs(jnp.asarray(dist))[:, None, :] * LOG2E
    shifted = ali[None] - shift[:, None, :, None]
    bias = jnp.where(jnp.asarray(valid)[None, :, :, None, :], shifted[:, None], NEG_INF)
    return bias.reshape(shift.shape[0], 3, 3 * WINDOW, WIN_LANES).astype(_F32)


def _prep_weights(norm_g, w_in, q_norm_g, k_norm_g, rpb, sink, mem_norm_g, w_mem_kv, w_out):
    depth = w_in.shape[0]
    sizes = [D_NA, D_NA, D_NA, D_NA, D_WIN, D_KV_WIN, D_KV_WIN, D_WIN, D_MEM, D_MEM]
    offs = np.concatenate([[0], np.cumsum(sizes)])
    seg = lambda i: w_in[:, :, offs[i]:offs[i + 1]]
    na_q, na_k, na_v, na_g, wq, wk, wv, wg, mq, mg = [seg(i) for i in range(10)]
    w_perm = jnp.concatenate([na_q, wq, mq, na_k, wk, na_v, wv, na_g, wg, mg], axis=-1)
    wt = jnp.swapaxes(w_perm, 1, 2).astype(_BF16)
    q_scale = HEAD_DIM ** -0.5 * LOG2E
    heads = (N_HEADS_NA, N_HEADS_WIN, N_HEADS_MEM)
    gq = jnp.concatenate([jnp.tile(q_norm_g[:, i], (1, n)) for i, n in enumerate(heads)], axis=-1) * q_scale
    gk = jnp.concatenate([jnp.tile(k_norm_g[:, 0], (1, N_HEADS_NA)), jnp.tile(k_norm_g[:, 1], (1, N_KV_WIN))], axis=-1)
    gkm = jnp.tile(k_norm_g[:, 2], (1, N_HEADS_MEM))
    sink2 = sink.astype(_F32) * LOG2E
    rpb2 = rpb.astype(_F32) * LOG2E

    q_bound = HEAD_DIM ** 0.5 * jnp.max(jnp.abs(q_norm_g), axis=-1) * q_scale
    k_bound = HEAD_DIM ** 0.5 * jnp.max(jnp.abs(k_norm_g), axis=-1)
    qk = q_bound * k_bound
    rpb_max, rpb_min = jnp.max(rpb2, axis=(2, 3)), jnp.min(rpb2, axis=(2, 3))
    shift_na = qk[:, 0:1] + rpb_max
    shift_win = jnp.maximum(qk[:, 1:2], sink2)
    shift_mem = jnp.broadcast_to(qk[:, 2:3], (depth, N_HEADS_MEM))
    slack = jnp.max(2.0 * qk + jnp.stack([jnp.max(rpb_max - rpb_min, axis=1), jnp.zeros((depth,), _F32),
                                          jnp.zeros((depth,), _F32)], axis=1), axis=1)
    bound_ok = slack <= MAX_SHIFT_SLACK
    use = bound_ok[:, None].astype(_F32)
    shift_na, shift_win, shift_mem = shift_na * use, shift_win * use, shift_mem * use
    sink_logit = jnp.repeat(sink2 - shift_win, WINDOW, axis=-1)
    return dict(
        ng=norm_g.reshape(depth, 1, D_MODEL), wt=wt,
        gq=gq.reshape(depth, D_MIX, 1), gk=gk.reshape(depth, D_K, 1), gkm=gkm.reshape(depth, D_MEM, 1),
        bna=_na_bias(rpb2, shift_na), bwin=_win_bias(shift_win),
        sink_rows=jnp.stack([sink_logit, jnp.exp2(sink_logit)], axis=1),
        bmem=jnp.broadcast_to(-shift_mem[:, :, None, None], (depth, N_HEADS_MEM, 1, TILE)),
        mg=mem_norm_g.reshape(depth, 1, D_MODEL),
        wmt=jnp.swapaxes(w_mem_kv, 1, 2).astype(_BF16),
        wo=w_out.astype(_BF16),
        bound_ok=bound_ok,
    )


def _forward(xs, mems, norm_g, w_in, q_norm_g, k_norm_g, rpb, sink, mem_norm_g, w_mem_kv, w_out):
    depth = w_in.shape[0]
    w = _prep_weights(norm_g, w_in, q_norm_g, k_norm_g, rpb, sink, mem_norm_g, w_mem_kv, w_out)
    km, vm = _memkv_call(jnp.concatenate(mems, axis=0), w["mg"], w["wmt"], w["gkm"])
    mem_off = np.concatenate([[0], np.cumsum([m.shape[0] for m in mems])])
    xs = list(xs)
    mixes = [None] * len(xs)
    tables = (w["bna"], w["bwin"], w["sink_rows"], w["bmem"])
    for l in range(depth):
        for gi in range(len(xs)):
            outs = _proj_call(l, xs[gi], w, mixes[gi])
            if mixes[gi] is not None:
                xs[gi], outs = outs[0], outs[1:]
            q, k, v, g = outs
            lo = int(mem_off[gi])
            mixes[gi] = lax.cond(w["bound_ok"][l],
                                 lambda *a: _attn_call(l, False, *a[:6], lo, *a[6:]),
                                 lambda *a: _attn_call(l, True, *a[:6], lo, *a[6:]),
                                 q, g, k, v, km, vm, *tables)
    return tuple(_proj_call(None, x, w, mix)[0] for x, mix in zip(xs, mixes))


def kernel(x_prompt, x_sample, mem_prompt, mem_sample, norm_g, w_in, q_norm_g, k_norm_g, rpb, sink, mem_norm_g, w_mem_kv, w_out):
    return _forward((x_prompt, x_sample), (mem_prompt, mem_sample), norm_g, w_in, q_norm_g, k_norm_g, rpb, sink,
                    mem_norm_g, w_mem_kv, w_out)
```

```python
import functools

import numpy as np
import jax
import jax.numpy as jnp
from jax import lax
from jax.experimental import pallas as pl
from jax.experimental.pallas import tpu as pltpu

D_MODEL = 1024
HEAD_DIM = 64
N_HEADS_NA = 6
N_HEADS_WIN = 6
N_KV_WIN = 2
N_HEADS_MEM = 4
D_NA = N_HEADS_NA * HEAD_DIM
D_WIN = N_HEADS_WIN * HEAD_DIM
D_KV_WIN = N_KV_WIN * HEAD_DIM
D_MEM = N_HEADS_MEM * HEAD_DIM
D_MIX = D_NA + D_WIN + D_MEM
D_K = D_NA + D_KV_WIN
D_PROJ = D_MIX + 2 * D_K + D_MIX
N_MEM = 256
GRID_W = 64
NA_ROWS = 8
NA_COLS = 16
WINDOW = 128
RMS_EPS = 1e-6
NEG_INF = -1e30
LOG2E = 1.4426950408889634
MAX_SHIFT_SLACK = 64.0

TILE = 256
TILE_ROWS = TILE // GRID_W
EXP_LAG = 1
PV_LAG = 6
TILES_PER_STEP = 4
WIN_G = N_HEADS_WIN // N_KV_WIN
WIN_LANES = N_KV_WIN * WIN_G * WINDOW
VMEM_LIMIT_BYTES = 56 * 1024 * 1024

_BF16 = jnp.bfloat16
_F32 = jnp.float32
_NT = (((1,), (1,)), ((), ()))
_ONES_ROWS = 16


def _rms_rows(x, gain):
    ms = jnp.mean(x * x, axis=-1, keepdims=True)
    return x * lax.rsqrt(ms + RMS_EPS) * gain


def _head_norm_cols(p, gain):
    ms = jnp.mean(p * p, axis=0, keepdims=True)
    return p * lax.rsqrt(ms + RMS_EPS) * gain


def _proj_kernel(has_out, has_in, *refs):
    refs = list(refs)
    x_ref = refs.pop(0)
    mix_ref, wo_ref = (refs.pop(0), refs.pop(0)) if has_out else (None, None)
    ng_ref, wt_ref, gq_ref, gk_ref = [refs.pop(0) for _ in range(4)] if has_in else [None] * 4
    xo_ref = refs.pop(0) if has_out else None
    q_out, k_out, v_out, g_out, xn_scr = refs if has_in else [None] * 5
    ta = x_ref.shape[0]
    n_tiles = ta // TILE
    chunk = 256
    x_src = xo_ref if has_out else x_ref

    def normalise(j):
        rows = slice(j * TILE, (j + 1) * TILE)
        xn_scr[:, rows] = _rms_rows(x_src[rows, :], ng_ref[...]).astype(_BF16).T

    def queries(c0, j, p):
        for hh in range(chunk // HEAD_DIM):
            r0 = c0 + hh * HEAD_DIM
            qh = _head_norm_cols(p[hh * HEAD_DIM:(hh + 1) * HEAD_DIM, :], gq_ref[r0:r0 + HEAD_DIM, :])
            q_out[j, r0:r0 + HEAD_DIM, :] = qh.astype(_BF16)

    def keys(c0, j, p):
        kn = jnp.concatenate(
            [_head_norm_cols(p[hh * HEAD_DIM:(hh + 1) * HEAD_DIM, :], gk_ref[c0 + hh * HEAD_DIM:c0 + (hh + 1) * HEAD_DIM, :])
             for hh in range(chunk // HEAD_DIM)], axis=0)
        k_out[j * TILE:(j + 1) * TILE, c0:c0 + chunk] = kn.T.astype(_BF16)

    def values(c0, j, p):
        v_out[j, c0:c0 + chunk, :] = p.astype(_BF16)

    def gates(c0, j, p):
        g_out[j, c0:c0 + chunk, :] = (p * jax.nn.sigmoid(p)).astype(_BF16)

    def residual(c0, j, y):
        rows = slice(j * TILE, (j + 1) * TILE)
        xo_ref[rows, c0:c0 + chunk] = x_ref[rows, c0:c0 + chunk] + y

    def in_items(j):
        sections = [(0, D_MIX, queries), (D_MIX, D_K, keys), (D_MIX + 2 * D_K, D_MIX, gates), (D_MIX + D_K, D_K, values)]
        return [(functools.partial(
                    lambda r0: jnp.dot(wt_ref[r0:r0 + chunk, :], xn_scr[:, j * TILE:(j + 1) * TILE],
                                       preferred_element_type=_F32), base + c0),
                 functools.partial(epilogue, c0, j))
                for base, rows, epilogue in sections for c0 in range(0, rows, chunk)]

    def out_items(j):
        return [(functools.partial(
                    lambda c0: jnp.dot(mix_ref[j * TILE:(j + 1) * TILE, :], wo_ref[:, c0:c0 + chunk],
                                       preferred_element_type=_F32), c0),
                 functools.partial(residual, c0, j))
                for c0 in range(0, D_MODEL, chunk)]

    order = []
    if has_out:
        order += out_items(0)
    if has_in:
        order.append(functools.partial(normalise, 0))
        for j in range(n_tiles):
            items = in_items(j)
            third = len(items) // 3
            order += items[:third]
            if j + 1 < n_tiles:
                if has_out:
                    order += out_items(j + 1)
                order += items[third:2 * third]
                order.append(functools.partial(normalise, j + 1))
                order += items[2 * third:]
            else:
                order += items[third:]
    else:
        for j in range(1, n_tiles):
            order += out_items(j)

    pending = None
    for entry in order:
        if callable(entry):
            if pending is not None:
                pending[0](pending[1])
                pending = None
            entry()
            continue
        matmul, epilogue = entry
        result = matmul()
        if pending is not None:
            pending[0](pending[1])
        pending = (epilogue, result)
    pending[0](pending[1])


def _proj_call(layer, x, w, mix=None):
    b, t, _ = x.shape
    has_out, has_in = mix is not None, layer is not None
    ta = next(c for c in (1024, 512, TILE) if t % c == 0)
    nt = t // TILE
    tpb = ta // TILE
    tokens = lambda cols: pl.BlockSpec((None, ta, cols), lambda i, j: (i, j, 0))
    tiles = lambda rows: pl.BlockSpec((None, tpb, rows, TILE), lambda i, j: (i, j, 0, 0))
    per_layer = lambda l, *shape: pl.BlockSpec((None,) + shape, lambda i, j: (l,) + (0,) * len(shape),
                                               pipeline_mode=pl.Buffered(1))
    operands, in_specs, out_specs, out_shape, scratch = [x], [tokens(D_MODEL)], [], [], []
    if has_out:
        prev = (w["wo"].shape[0] - 1) if layer is None else layer - 1
        operands += [mix, w["wo"]]
        in_specs += [tokens(D_MIX), per_layer(prev, D_MIX, D_MODEL)]
        out_specs.append(tokens(D_MODEL))
        out_shape.append(jax.ShapeDtypeStruct(x.shape, x.dtype))
    if has_in:
        operands += [w["ng"], w["wt"], w["gq"], w["gk"]]
        in_specs += [per_layer(layer, 1, D_MODEL), per_layer(layer, D_PROJ, D_MODEL), per_layer(layer, D_MIX, 1),
                     per_layer(layer, D_K, 1)]
        out_specs += [tiles(D_MIX), tokens(D_K), tiles(D_K), tiles(D_MIX)]
        out_shape += [
            jax.ShapeDtypeStruct((b, nt, D_MIX, TILE), _BF16),
            jax.ShapeDtypeStruct((b, t, D_K), _BF16),
            jax.ShapeDtypeStruct((b, nt, D_K, TILE), _BF16),
            jax.ShapeDtypeStruct((b, nt, D_MIX, TILE), _BF16),
        ]
        scratch.append(pltpu.VMEM((D_MODEL, ta), _BF16))
    return pl.pallas_call(
        functools.partial(_proj_kernel, has_out, has_in),
        grid=(b, t // ta),
        in_specs=in_specs, out_specs=out_specs, out_shape=out_shape, scratch_shapes=scratch,
        compiler_params=pltpu.CompilerParams(
            dimension_semantics=("arbitrary", "arbitrary"), vmem_limit_bytes=VMEM_LIMIT_BYTES),
        name="proj",
    )(*operands)


def _memkv_kernel(mem_ref, g_ref, wt_ref, gk_ref, km_out, vm_out):
    xn = _rms_rows(mem_ref[...], g_ref[...]).astype(_BF16)
    p = lax.dot_general(wt_ref[...], xn, _NT, preferred_element_type=_F32)
    kn = jnp.concatenate(
        [_head_norm_cols(p[h * HEAD_DIM:(h + 1) * HEAD_DIM, :], gk_ref[h * HEAD_DIM:(h + 1) * HEAD_DIM, :])
         for h in range(N_HEADS_MEM)], axis=0)
    km_out[...] = kn.T.astype(_BF16)
    vm_out[...] = p[D_MEM:, :].astype(_BF16)


def _memkv_call(mem, g, wt, gk):
    bm = mem.shape[0]
    depth = g.shape[0]
    return pl.pallas_call(
        _memkv_kernel,
        grid=(depth, bm),
        in_specs=[
            pl.BlockSpec((None, N_MEM, D_MODEL), lambda l, i: (i, 0, 0)),
            pl.BlockSpec((None, 1, D_MODEL), lambda l, i: (l, 0, 0)),
            pl.BlockSpec((None, 2 * D_MEM, D_MODEL), lambda l, i: (l, 0, 0)),
            pl.BlockSpec((None, D_MEM, 1), lambda l, i: (l, 0, 0)),
        ],
        out_specs=[
            pl.BlockSpec((None, None, N_MEM, D_MEM), lambda l, i: (l, i, 0, 0)),
            pl.BlockSpec((None, None, D_MEM, N_MEM), lambda l, i: (l, i, 0, 0)),
        ],
        out_shape=[
            jax.ShapeDtypeStruct((depth, bm, N_MEM, D_MEM), _BF16),
            jax.ShapeDtypeStruct((depth, bm, D_MEM, N_MEM), _BF16),
        ],
        compiler_params=pltpu.CompilerParams(
            dimension_semantics=("arbitrary", "arbitrary"), vmem_limit_bytes=VMEM_LIMIT_BYTES),
        name="memkv",
    )(mem, g, wt, gk)


def _pad_head(q, odd):
    z = jnp.zeros_like(q)
    return jnp.concatenate([z, q] if odd else [q, z], axis=0)


class _Unit:
    def __init__(self, k_parts, rhs, bias_of, v_groups, sink, done):
        self.k_parts, self.rhs, self.bias_of = k_parts, rhs, bias_of
        self.v_groups, self.sink, self.done = v_groups, sink, done
        self.rows = [0]
        self._rhs = self.l = None
        self.accs = [None] * len(v_groups)

    def scores(self, j):
        if self._rhs is None:
            self._rhs = self.rhs()
        k_blk = self.k_parts[j]()
        r0, r = self.rows[j], k_blk.shape[0]
        if len(self.rows) == j + 1:
            self.rows.append(r0 + r)
        return jnp.dot(k_blk, self._rhs, preferred_element_type=_F32) + self.bias_of(r0, r)

    def probs(self, s):
        return jnp.exp2(s).astype(_BF16)

    def weigh(self, j, pb):
        for gi, (ls, vs) in enumerate(self.v_groups):
            v = vs[j]()
            v1 = jnp.concatenate([v, jnp.ones((_ONES_ROWS, v.shape[1]), v.dtype)], axis=0)
            d = jnp.dot(v1, pb[:, ls], preferred_element_type=_F32)
            self.accs[gi] = d if self.accs[gi] is None else self.accs[gi] + d

    def finish(self, sink_term=None):
        if self.sink is not None and sink_term is None:
            sink_term = self.sink()[1:2, :]
        outs = []
        for acc, (ls, _) in zip(self.accs, self.v_groups):
            l = acc[HEAD_DIM:HEAD_DIM + 1, :]
            if sink_term is not None:
                l = l + sink_term[:, ls]
            outs.append(acc[:HEAD_DIM, :] * (1.0 / l))
        self.done(outs)

    def run_exact(self, scr):
        m = None
        for j in range(len(self.k_parts)):
            s = self.scores(j)
            scr[self.rows[j]:self.rows[j + 1], :] = s
            mj = jnp.max(s, axis=0, keepdims=True)
            m = mj if m is None else jnp.maximum(m, mj)
        sink = None if self.sink is None else self.sink()[0:1, :]
        if sink is not None:
            m = jnp.maximum(m, sink)
        for j in range(len(self.k_parts)):
            self.weigh(j, self.probs(scr[self.rows[j]:self.rows[j + 1], :] - m))
        self.finish(None if sink is None else jnp.exp2(sink - m))


def _attn_kernel(exact_max, tps, q_ref, g_ref, kp_ref, kc_ref, kn_ref, vp_ref, vc_ref, vn_ref, km_ref, vm_ref,
                 bna_first_ref, bna_last_ref, bna_mid_ref, bwin_ref, sink_ref, bmem_ref, o_ref, mix_scr, *score_scr):
    n = pl.program_id(0)
    n_steps = pl.num_programs(0)
    wb = WINDOW
    full = slice(0, TILE)
    scr_na, scr_win = score_scr if exact_max else (None, None)

    def k_rows(tile, r0, r, lanes):
        if tile < 0:
            return lambda: kp_ref[r0:r0 + r, lanes]
        if tile >= tps:
            return lambda: kn_ref[r0:r0 + r, lanes]
        return lambda: kc_ref[tile * TILE + r0:tile * TILE + r0 + r, lanes]

    def v_cols(tile, rows, c0, c):
        if tile < 0:
            return lambda: vp_ref[rows, c0:c0 + c]
        if tile >= tps:
            return lambda: vn_ref[rows, c0:c0 + c]
        return lambda: vc_ref[tile, rows, c0:c0 + c]

    def gate_store(i, r0, lanes, o):
        gate = g_ref[i, r0:r0 + HEAD_DIM, lanes].astype(_F32)
        mix_scr[i, r0:r0 + HEAD_DIM, lanes] = (o * gate).astype(_BF16)

    def na_unit(i, h):
        hrows = slice(h * HEAD_DIM, (h + 1) * HEAD_DIM)
        lanes = slice(128 * (h // 2), 128 * (h // 2) + 128)
        bias_ref = bna_first_ref if i == 0 else bna_last_ref if i == tps - 1 else bna_mid_ref
        return _Unit(
            [k_rows(i + d, 0, TILE, lanes) for d in (-1, 0, 1)],
            lambda: _pad_head(q_ref[i, hrows, :], h % 2),
            lambda r0, r: bias_ref[h, r0:r0 + r, :],
            [(full, [v_cols(i + d, hrows, 0, TILE) for d in (-1, 0, 1)])], None,
            lambda outs: gate_store(i, h * HEAD_DIM, full, outs[0]))

    def win_unit(i, blk):
        qlanes = slice(blk * wb, (blk + 1) * wb)
        per_tile = TILE // wb
        src = [divmod(i * per_tile + blk + d, per_tile) for d in (-1, 0, 1)]
        src = [(t, b * wb) for t, b in src]
        if i == 0 and blk == 0:
            variant = jnp.where(n == 0, 0, 1)
        elif i == tps - 1 and blk == per_tile - 1:
            variant = jnp.where(n == n_steps - 1, 2, 1)
        else:
            variant = 1

        def rhs():
            rows = []
            for kv in range(N_KV_WIN):
                q_cat = jnp.concatenate(
                    [q_ref[i, D_NA + (kv * WIN_G + g) * HEAD_DIM:D_NA + (kv * WIN_G + g + 1) * HEAD_DIM, qlanes]
                     for g in range(WIN_G)], axis=1)
                z = jnp.zeros_like(q_cat)
                rows.append(jnp.concatenate([z, q_cat] if kv else [q_cat, z], axis=1))
            return jnp.concatenate(rows, axis=0)

        def done(outs):
            for kv, o in enumerate(outs):
                for g in range(WIN_G):
                    gate_store(i, D_NA + (kv * WIN_G + g) * HEAD_DIM, qlanes, o[:, g * wb:(g + 1) * wb])

        groups = []
        for kv in range(N_KV_WIN):
            vrows = slice(D_NA + kv * HEAD_DIM, D_NA + (kv + 1) * HEAD_DIM)
            groups.append((slice(kv * WIN_G * wb, (kv + 1) * WIN_G * wb), [v_cols(t, vrows, r0, wb) for t, r0 in src]))
        return _Unit(
            [k_rows(t, r0, wb, slice(D_NA, D_NA + D_KV_WIN)) for t, r0 in src],
            rhs, lambda r0, r: bwin_ref[variant, r0:r0 + r, :], groups, lambda: sink_ref[...], done)

    def mem_unit(i, h):
        r0 = D_NA + D_WIN + h * HEAD_DIM
        lanes = slice(128 * (h // 2), 128 * (h // 2) + 128)
        return _Unit(
            [lambda: km_ref[:, lanes]],
            lambda: _pad_head(q_ref[i, r0:r0 + HEAD_DIM, :], h % 2),
            lambda r0_, r: bmem_ref[h],
            [(full, [lambda: vm_ref[h * HEAD_DIM:(h + 1) * HEAD_DIM, :]])], None,
            lambda outs: gate_store(i, r0, full, outs[0]))

    def emit_chunk(i, c):
        cs = slice(c * 256, (c + 1) * 256)
        o_ref[i * TILE:(i + 1) * TILE, cs] = mix_scr[i, cs, :].T

    units, ready = [], {}
    for i in range(tps):
        base = len(units)
        units += ([mem_unit(i, h) for h in range(N_HEADS_MEM)] + [win_unit(i, b) for b in range(TILE // wb)]
                  + [na_unit(i, h) for h in range(N_HEADS_NA)])
        for after, chunks in {3: [3], 5: [2], 9: [0], 11: [1]}.items():
            ready[base + after] = [functools.partial(emit_chunk, i, c) for c in chunks]
    pending = []

    if exact_max:
        for ui, unit in enumerate(units):
            unit.run_exact(scr_win if unit.sink is not None else scr_na)
            for emit in ready.get(ui, []):
                emit()
        return

    blocks = [(ui, j) for ui, unit in enumerate(units) for j in range(len(unit.k_parts))]
    s_q, p_q = {}, {}
    for t in range(len(blocks) + PV_LAG):
        if t < len(blocks):
            ui, j = blocks[t]
            s_q[t] = units[ui].scores(j)
        if pending:
            pending.pop(0)()
        if 0 <= t - EXP_LAG < len(blocks):
            ui, j = blocks[t - EXP_LAG]
            p_q[t - EXP_LAG] = units[ui].probs(s_q.pop(t - EXP_LAG))
        if 0 <= t - PV_LAG < len(blocks):
            ui, j = blocks[t - PV_LAG]
            units[ui].weigh(j, p_q.pop(t - PV_LAG))
            if j == len(units[ui].k_parts) - 1:
                units[ui].finish()
                pending.extend(ready.get(ui, []))
    while pending:
        pending.pop(0)()


def _attn_call(layer, exact_max, q, g, k, v, km, vm, bna, bwin, sink_rows, bmem):
    b, t, _ = k.shape
    nt = t // TILE
    tps = next(c for c in (TILES_PER_STEP, 3, 2) if nt % c == 0)
    steps = nt // tps
    last = steps - 1
    step_tiles = lambda rows: pl.BlockSpec((None, tps, rows, TILE), lambda n, i: (i, n, 0, 0))
    one_tile = lambda rows, tile_of: pl.BlockSpec((None, None, rows, TILE), lambda n, i: (i, tile_of(n), 0, 0))
    prev_tile = lambda n: jnp.maximum(n * tps - 1, 0)
    next_tile = lambda n: jnp.minimum((n + 1) * tps, nt - 1)
    const = lambda shape, index: pl.BlockSpec(shape, index, pipeline_mode=pl.Buffered(1))
    na_table = lambda variant_of: const((None, None, N_HEADS_NA, 3 * TILE, TILE),
                                        lambda n, i: (layer, variant_of(n), 0, 0, 0))
    scratch = [pltpu.VMEM((tps, D_MIX, TILE), _BF16)]
    if exact_max:
        scratch += [pltpu.VMEM((3 * TILE, TILE), _F32), pltpu.VMEM((3 * WINDOW, WIN_LANES), _F32)]
    return pl.pallas_call(
        functools.partial(_attn_kernel, exact_max, tps),
        grid=(steps, b),
        in_specs=[
            step_tiles(D_MIX),
            step_tiles(D_MIX),
            pl.BlockSpec((None, TILE, D_K), lambda n, i: (i, prev_tile(n), 0)),
            pl.BlockSpec((None, tps * TILE, D_K), lambda n, i: (i, n, 0)),
            pl.BlockSpec((None, TILE, D_K), lambda n, i: (i, next_tile(n), 0)),
            one_tile(D_K, prev_tile),
            step_tiles(D_K),
            one_tile(D_K, next_tile),
            pl.BlockSpec((None, None, N_MEM, D_MEM), lambda n, i: (layer, i, 0, 0)),
            pl.BlockSpec((None, None, D_MEM, N_MEM), lambda n, i: (layer, i, 0, 0)),
            na_table(lambda n: jnp.where(n == 0, 0, 1)),
            na_table(lambda n: jnp.where(n == last, 2, 1)),
            na_table(lambda n: 1),
            const((None, 3, 3 * WINDOW, WIN_LANES), lambda n, i: (layer, 0, 0, 0)),
            pl.BlockSpec((None, 2, WIN_LANES), lambda n, i: (layer, 0, 0)),
            pl.BlockSpec((None, N_HEADS_MEM, 1, TILE), lambda n, i: (layer, 0, 0, 0)),
        ],
        out_specs=pl.BlockSpec((None, tps * TILE, D_MIX), lambda n, i: (i, n, 0)),
        out_shape=jax.ShapeDtypeStruct((b, t, D_MIX), _BF16),
        scratch_shapes=scratch,
        compiler_params=pltpu.CompilerParams(
            dimension_semantics=("arbitrary", "arbitrary"), vmem_limit_bytes=VMEM_LIMIT_BYTES),
        name="attn_exact" if exact_max else "attn",
    )(q, g, k, k, k, v, v, v, km, vm, bna, bna, bna, bwin, sink_rows, bmem)


def _na_index_tables():
    n_tiles = 4
    rows = n_tiles * TILE_ROWS
    rel_rows, row_ok = [], []
    for n in (0, 1, n_tiles - 1):
        qr = n * TILE_ROWS + np.arange(TILE_ROWS)
        kr = (n - 1) * TILE_ROWS + np.arange(3 * TILE_ROWS)
        r0 = np.clip(qr - NA_ROWS // 2, 0, rows - NA_ROWS)
        ok = ((kr[:, None] >= r0[None, :]) & (kr[:, None] < r0[None, :] + NA_ROWS)
              & (kr[:, None] >= 0) & (kr[:, None] < rows))
        rel_rows.append(np.clip(kr[:, None] - qr[None, :] + NA_ROWS - 1, 0, 2 * NA_ROWS - 2))
        row_ok.append(ok)
    qc = np.arange(GRID_W)
    kc = np.arange(GRID_W)
    cs = np.clip(qc - NA_COLS // 2, 0, GRID_W - NA_COLS)
    col_ok = (kc[:, None] >= cs[None, :]) & (kc[:, None] < cs[None, :] + NA_COLS)
    rel_col = np.clip(kc[:, None] - qc[None, :] + NA_COLS - 1, 0, 2 * NA_COLS - 2)
    return np.stack(rel_rows), np.stack(row_ok), rel_col, col_ok


def _na_table_kernel(rel_rows, row_ok, t1_ref, out_ref):
    left_half = lax.broadcasted_iota(jnp.int32, (GRID_W, 2 * GRID_W), 1) < GRID_W
    masked = jnp.full((GRID_W, 2 * GRID_W), NEG_INF, _F32)
    for v in range(3):
        for kr in range(3 * TILE_ROWS):
            for pair in range(TILE_ROWS // 2):
                halves = [t1_ref[int(rel_rows[v, kr, qr])] if row_ok[v, kr, qr] else masked
                          for qr in (2 * pair, 2 * pair + 1)]
                out_ref[v, kr * GRID_W:(kr + 1) * GRID_W, pair * 2 * GRID_W:(pair + 1) * 2 * GRID_W] = (
                    jnp.where(left_half, halves[0], halves[1]))


def _na_bias(rpb2, shift):
    depth = rpb2.shape[0]
    rel_rows, row_ok, rel_col, col_ok = _na_index_tables()
    onehot = (rel_col.reshape(-1)[None, :] == np.arange(2 * NA_COLS - 1)[:, None]).astype(np.float32)
    t1 = jnp.einsum("lhab,bn->lhan", rpb2 - shift[:, :, None, None], jnp.asarray(onehot),
                    precision=lax.Precision.HIGHEST)
    t1 = jnp.where(jnp.asarray(col_ok.reshape(-1)), t1, NEG_INF).reshape(depth, N_HEADS_NA, 2 * NA_ROWS - 1, GRID_W, GRID_W)
    t1 = jnp.concatenate([t1, t1], axis=-1)
    return pl.pallas_call(
        functools.partial(_na_table_kernel, rel_rows, row_ok),
        grid=(depth, N_HEADS_NA),
        in_specs=[pl.BlockSpec((None, None, 2 * NA_ROWS - 1, GRID_W, 2 * GRID_W), lambda l, h: (l, h, 0, 0, 0))],
        out_specs=pl.BlockSpec((None, 3, None, 3 * TILE, TILE), lambda l, h: (l, 0, h, 0, 0)),
        out_shape=jax.ShapeDtypeStruct((depth, 3, N_HEADS_NA, 3 * TILE, TILE), _F32),
        compiler_params=pltpu.CompilerParams(
            dimension_semantics=("arbitrary", "arbitrary"), vmem_limit_bytes=VMEM_LIMIT_BYTES),
        name="na_table",
    )(t1)


def _win_bias(shift):
    i = np.arange(WINDOW)
    w = np.arange(3 * WINDOW)
    dist = (i[None, :] - w[:, None] + WINDOW).astype(np.float32)
    band = np.abs(dist) <= WINDOW
    edge = np.stack([w >= WINDOW, w >= 0, w < 2 * WINDOW])
    valid = band[None] & edge[:, :, None]
    slopes = 2.0 ** (-8.0 * jnp.arange(1, N_HEADS_WIN + 1, dtype=_F32) / N_HEADS_WIN)
    ali = -slopes[None, :, None] * jnp.abs(jnp.asarray(dist))[:, None, :] * LOG2E
    shifted = ali[None] - shift[:, None, :, None]
    bias = jnp.where(jnp.asarray(valid)[None, :, :, None, :], shifted[:, None], NEG_INF)
    return bias.reshape(shift.shape[0], 3, 3 * WINDOW, WIN_LANES).astype(_F32)


def _prep_weights(norm_g, w_in, q_norm_g, k_norm_g, rpb, sink, mem_norm_g, w_mem_kv, w_out):
    depth = w_in.shape[0]
    sizes = [D_NA, D_NA, D_NA, D_NA, D_WIN, D_KV_WIN, D_KV_WIN, D_WIN, D_MEM, D_MEM]
    offs = np.concatenate([[0], np.cumsum(sizes)])
    seg = lambda i: w_in[:, :, offs[i]:offs[i + 1]]
    na_q, na_k, na_v, na_g, wq, wk, wv, wg, mq, mg = [seg(i) for i in range(10)]
    w_perm = jnp.concatenate([na_q, wq, mq, na_k, wk, na_v, wv, na_g, wg, mg], axis=-1)
    wt = jnp.swapaxes(w_perm, 1, 2).astype(_BF16)
    q_scale = HEAD_DIM ** -0.5 * LOG2E
    heads = (N_HEADS_NA, N_HEADS_WIN, N_HEADS_MEM)
    gq = jnp.concatenate([jnp.tile(q_norm_g[:, i], (1, n)) for i, n in enumerate(heads)], axis=-1) * q_scale
    gk = jnp.concatenate([jnp.tile(k_norm_g[:, 0], (1, N_HEADS_NA)), jnp.tile(k_norm_g[:, 1], (1, N_KV_WIN))], axis=-1)
    gkm = jnp.tile(k_norm_g[:, 2], (1, N_HEADS_MEM))
    sink2 = sink.astype(_F32) * LOG2E
    rpb2 = rpb.astype(_F32) * LOG2E

    q_bound = HEAD_DIM ** 0.5 * jnp.max(jnp.abs(q_norm_g), axis=-1) * q_scale
    k_bound = HEAD_DIM ** 0.5 * jnp.max(jnp.abs(k_norm_g), axis=-1)
    qk = q_bound * k_bound
    rpb_max, rpb_min = jnp.max(rpb2, axis=(2, 3)), jnp.min(rpb2, axis=(2, 3))
    shift_na = qk[:, 0:1] + rpb_max
    shift_win = jnp.maximum(qk[:, 1:2], sink2)
    shift_mem = jnp.broadcast_to(qk[:, 2:3], (depth, N_HEADS_MEM))
    slack = jnp.max(2.0 * qk + jnp.stack([jnp.max(rpb_max - rpb_min, axis=1), jnp.zeros((depth,), _F32),
                                          jnp.zeros((depth,), _F32)], axis=1), axis=1)
    bound_ok = slack <= MAX_SHIFT_SLACK
    use = bound_ok[:, None].astype(_F32)
    shift_na, shift_win, shift_mem = shift_na * use, shift_win * use, shift_mem * use
    sink_logit = jnp.repeat(sink2 - shift_win, WINDOW, axis=-1)
    return dict(
        ng=norm_g.reshape(depth, 1, D_MODEL), wt=wt,
        gq=gq.reshape(depth, D_MIX, 1), gk=gk.reshape(depth, D_K, 1), gkm=gkm.reshape(depth, D_MEM, 1),
        bna=_na_bias(rpb2, shift_na), bwin=_win_bias(shift_win),
        sink_rows=jnp.stack([sink_logit, jnp.exp2(sink_logit)], axis=1),
        bmem=jnp.broadcast_to(-shift_mem[:, :, None, None], (depth, N_HEADS_MEM, 1, TILE)),
        mg=mem_norm_g.reshape(depth, 1, D_MODEL),
        wmt=jnp.swapaxes(w_mem_kv, 1, 2).astype(_BF16),
        wo=w_out.astype(_BF16),
        bound_ok=bound_ok,
    )


def _forward(xs, mems, norm_g, w_in, q_norm_g, k_norm_g, rpb, sink, mem_norm_g, w_mem_kv, w_out):
    depth = w_in.shape[0]
    w = _prep_weights(norm_g, w_in, q_norm_g, k_norm_g, rpb, sink, mem_norm_g, w_mem_kv, w_out)
    mem_kv = [_memkv_call(mem, w["mg"], w["wmt"], w["gkm"]) for mem in mems]
    xs = list(xs)
    mixes = [None] * len(xs)
    tables = (w["bna"], w["bwin"], w["sink_rows"], w["bmem"])
    for l in range(depth):
        for gi in range(len(xs)):
            outs = _proj_call(l, xs[gi], w, mixes[gi])
            if mixes[gi] is not None:
                xs[gi], outs = outs[0], outs[1:]
            q, k, v, g = outs
            mixes[gi] = lax.cond(w["bound_ok"][l],
                                 functools.partial(_attn_call, l, False),
                                 functools.partial(_attn_call, l, True),
                                 q, g, k, v, *mem_kv[gi], *tables)
    return tuple(_proj_call(None, x, w, mix)[0] for x, mix in zip(xs, mixes))


def kernel(x_prompt, x_sample, mem_prompt, mem_sample, norm_g, w_in, q_norm_g, k_norm_g, rpb, sink, mem_norm_g, w_mem_kv, w_out):
    return _forward((x_prompt, x_sample), (mem_prompt, mem_sample), norm_g, w_in, q_norm_g, k_norm_g, rpb, sink,
                    mem_norm_g, w_mem_kv, w_out)
```

```python
import functools

import numpy as np
import jax
import jax.numpy as jnp
from jax import lax
from jax.experimental import pallas as pl
from jax.experimental.pallas import tpu as pltpu

D_MODEL = 1024
HEAD_DIM = 64
N_HEADS_NA = 6
N_HEADS_WIN = 6
N_KV_WIN = 2
N_HEADS_MEM = 4
D_NA = N_HEADS_NA * HEAD_DIM
D_WIN = N_HEADS_WIN * HEAD_DIM
D_KV_WIN = N_KV_WIN * HEAD_DIM
D_MEM = N_HEADS_MEM * HEAD_DIM
D_MIX = D_NA + D_WIN + D_MEM
D_K = D_NA + D_KV_WIN
D_PROJ = D_MIX + 2 * D_K + D_MIX
N_MEM = 256
GRID_W = 64
NA_ROWS = 8
NA_COLS = 16
WINDOW = 128
RMS_EPS = 1e-6
NEG_INF = -1e30
LOG2E = 1.4426950408889634
MAX_SHIFT_SLACK = 64.0

TILE = 256
TILE_ROWS = TILE // GRID_W
PROJ_LAG = 1
EXP_LAG = 1
PV_LAG = 6
TILES_PER_STEP = 4
WIN_G = N_HEADS_WIN // N_KV_WIN
WIN_LANES = N_KV_WIN * WIN_G * WINDOW
VMEM_LIMIT_BYTES = 56 * 1024 * 1024

_BF16 = jnp.bfloat16
_F32 = jnp.float32
_NT = (((1,), (1,)), ((), ()))
_ONES_ROWS = 16


def _rms_rows(x, gain):
    ms = jnp.mean(x * x, axis=-1, keepdims=True)
    return x * lax.rsqrt(ms + RMS_EPS) * gain


def _head_norm_cols(p, gain):
    ms = jnp.mean(p * p, axis=0, keepdims=True)
    return p * lax.rsqrt(ms + RMS_EPS) * gain


def _proj_kernel(has_out, has_in, *refs):
    refs = list(refs)
    x_ref = refs.pop(0)
    mix_ref, wo_ref = (refs.pop(0), refs.pop(0)) if has_out else (None, None)
    ng_ref, wt_ref, gq_ref, gk_ref = [refs.pop(0) for _ in range(4)] if has_in else [None] * 4
    xo_ref = refs.pop(0) if has_out else None
    q_out, k_out, v_out, g_out, xn_scr = refs if has_in else [None] * 5
    ta = x_ref.shape[0]
    n_tiles = ta // TILE
    chunk = 256
    x_src = xo_ref if has_out else x_ref

    def normalise(j):
        rows = slice(j * TILE, (j + 1) * TILE)
        xn_scr[:, rows] = _rms_rows(x_src[rows, :], ng_ref[...]).astype(_BF16).T

    def queries(c0, j, p):
        for hh in range(chunk // HEAD_DIM):
            r0 = c0 + hh * HEAD_DIM
            qh = _head_norm_cols(p[hh * HEAD_DIM:(hh + 1) * HEAD_DIM, :], gq_ref[r0:r0 + HEAD_DIM, :])
            q_out[j, r0:r0 + HEAD_DIM, :] = qh.astype(_BF16)

    def keys(c0, j, p):
        kn = jnp.concatenate(
            [_head_norm_cols(p[hh * HEAD_DIM:(hh + 1) * HEAD_DIM, :], gk_ref[c0 + hh * HEAD_DIM:c0 + (hh + 1) * HEAD_DIM, :])
             for hh in range(chunk // HEAD_DIM)], axis=0)
        k_out[j * TILE:(j + 1) * TILE, c0:c0 + chunk] = kn.T.astype(_BF16)

    def values(c0, j, p):
        v_out[j, c0:c0 + chunk, :] = p.astype(_BF16)

    def gates(c0, j, p):
        g_out[j, c0:c0 + chunk, :] = (p * jax.nn.sigmoid(p)).astype(_BF16)

    def residual(c0, j, y):
        rows = slice(j * TILE, (j + 1) * TILE)
        xo_ref[rows, c0:c0 + chunk] = x_ref[rows, c0:c0 + chunk] + y

    def in_items(j):
        sections = [(0, D_MIX, queries), (D_MIX, D_K, keys), (D_MIX + 2 * D_K, D_MIX, gates), (D_MIX + D_K, D_K, values)]
        return [(functools.partial(
                    lambda r0: jnp.dot(wt_ref[r0:r0 + chunk, :], xn_scr[:, j * TILE:(j + 1) * TILE],
                                       preferred_element_type=_F32), base + c0),
                 functools.partial(epilogue, c0, j))
                for base, rows, epilogue in sections for c0 in range(0, rows, chunk)]

    def out_items(j):
        return [(functools.partial(
                    lambda c0: jnp.dot(mix_ref[j * TILE:(j + 1) * TILE, :], wo_ref[:, c0:c0 + chunk],
                                       preferred_element_type=_F32), c0),
                 functools.partial(residual, c0, j))
                for c0 in range(0, D_MODEL, chunk)]

    order = []
    if has_out:
        order += out_items(0)
    if has_in:
        order.append(functools.partial(normalise, 0))
        for j in range(n_tiles):
            items = in_items(j)
            third = len(items) // 3
            order += items[:third]
            if j + 1 < n_tiles:
                if has_out:
                    order += out_items(j + 1)
                order += items[third:2 * third]
                order.append(functools.partial(normalise, j + 1))
                order += items[2 * third:]
            else:
                order += items[third:]
    else:
        for j in range(1, n_tiles):
            order += out_items(j)

    pending = []
    for entry in order:
        if callable(entry):
            while pending:
                epilogue, result = pending.pop(0)
                epilogue(result)
            entry()
            continue
        matmul, epilogue = entry
        pending.append((epilogue, matmul()))
        while len(pending) > PROJ_LAG:
            epilogue, result = pending.pop(0)
            epilogue(result)
    while pending:
        epilogue, result = pending.pop(0)
        epilogue(result)


def _proj_call(layer, x, w, mix=None):
    b, t, _ = x.shape
    has_out, has_in = mix is not None, layer is not None
    ta = next(c for c in (1024, 512, TILE) if t % c == 0)
    nt = t // TILE
    tpb = ta // TILE
    tokens = lambda cols: pl.BlockSpec((None, ta, cols), lambda i, j: (i, j, 0))
    tiles = lambda rows: pl.BlockSpec((None, tpb, rows, TILE), lambda i, j: (i, j, 0, 0))
    per_layer = lambda l, *shape: pl.BlockSpec((None,) + shape, lambda i, j: (l,) + (0,) * len(shape),
                                               pipeline_mode=pl.Buffered(1))
    operands, in_specs, out_specs, out_shape, scratch = [x], [tokens(D_MODEL)], [], [], []
    if has_out:
        prev = (w["wo"].shape[0] - 1) if layer is None else layer - 1
        operands += [mix, w["wo"]]
        in_specs += [tokens(D_MIX), per_layer(prev, D_MIX, D_MODEL)]
        out_specs.append(tokens(D_MODEL))
        out_shape.append(jax.ShapeDtypeStruct(x.shape, x.dtype))
    if has_in:
        operands += [w["ng"], w["wt"], w["gq"], w["gk"]]
        in_specs += [per_layer(layer, 1, D_MODEL), per_layer(layer, D_PROJ, D_MODEL), per_layer(layer, D_MIX, 1),
                     per_layer(layer, D_K, 1)]
        out_specs += [tiles(D_MIX), tokens(D_K), tiles(D_K), tiles(D_MIX)]
        out_shape += [
            jax.ShapeDtypeStruct((b, nt, D_MIX, TILE), _BF16),
            jax.ShapeDtypeStruct((b, t, D_K), _BF16),
            jax.ShapeDtypeStruct((b, nt, D_K, TILE), _BF16),
            jax.ShapeDtypeStruct((b, nt, D_MIX, TILE), _BF16),
        ]
        scratch.append(pltpu.VMEM((D_MODEL, ta), _BF16))
    return pl.pallas_call(
        functools.partial(_proj_kernel, has_out, has_in),
        grid=(b, t // ta),
        in_specs=in_specs, out_specs=out_specs, out_shape=out_shape, scratch_shapes=scratch,
        compiler_params=pltpu.CompilerParams(
            dimension_semantics=("arbitrary", "arbitrary"), vmem_limit_bytes=VMEM_LIMIT_BYTES),
        name="proj",
    )(*operands)


def _memkv_kernel(mem_ref, g_ref, wt_ref, gk_ref, km_out, vm_out):
    xn = _rms_rows(mem_ref[...], g_ref[...]).astype(_BF16)
    p = lax.dot_general(wt_ref[...], xn, _NT, preferred_element_type=_F32)
    kn = jnp.concatenate(
        [_head_norm_cols(p[h * HEAD_DIM:(h + 1) * HEAD_DIM, :], gk_ref[h * HEAD_DIM:(h + 1) * HEAD_DIM, :])
         for h in range(N_HEADS_MEM)], axis=0)
    km_out[...] = kn.T.astype(_BF16)
    vm_out[...] = p[D_MEM:, :].astype(_BF16)


def _memkv_call(mem, g, wt, gk):
    bm = mem.shape[0]
    depth = g.shape[0]
    return pl.pallas_call(
        _memkv_kernel,
        grid=(depth, bm),
        in_specs=[
            pl.BlockSpec((None, N_MEM, D_MODEL), lambda l, i: (i, 0, 0)),
            pl.BlockSpec((None, 1, D_MODEL), lambda l, i: (l, 0, 0)),
            pl.BlockSpec((None, 2 * D_MEM, D_MODEL), lambda l, i: (l, 0, 0)),
            pl.BlockSpec((None, D_MEM, 1), lambda l, i: (l, 0, 0)),
        ],
        out_specs=[
            pl.BlockSpec((None, None, N_MEM, D_MEM), lambda l, i: (l, i, 0, 0)),
            pl.BlockSpec((None, None, D_MEM, N_MEM), lambda l, i: (l, i, 0, 0)),
        ],
        out_shape=[
            jax.ShapeDtypeStruct((depth, bm, N_MEM, D_MEM), _BF16),
            jax.ShapeDtypeStruct((depth, bm, D_MEM, N_MEM), _BF16),
        ],
        compiler_params=pltpu.CompilerParams(
            dimension_semantics=("arbitrary", "arbitrary"), vmem_limit_bytes=VMEM_LIMIT_BYTES),
        name="memkv",
    )(mem, g, wt, gk)


def _pad_head(q, odd):
    z = jnp.zeros_like(q)
    return jnp.concatenate([z, q] if odd else [q, z], axis=0)


class _Unit:
    def __init__(self, k_parts, rhs, bias_of, v_groups, sink, done, merge_pv=False):
        self.k_parts, self.rhs, self.bias_of = k_parts, rhs, bias_of
        self.v_groups, self.sink, self.done = v_groups, sink, done
        self.merge_pv = merge_pv
        self.rows = [0]
        self._rhs = None
        self._pbs = []
        self.accs = [None] * len(v_groups)

    def scores(self, j):
        if self._rhs is None:
            self._rhs = self.rhs()
        k_blk = self.k_parts[j]()
        r0, r = self.rows[j], k_blk.shape[0]
        if len(self.rows) == j + 1:
            self.rows.append(r0 + r)
        return jnp.dot(k_blk, self._rhs, preferred_element_type=_F32) + self.bias_of(r0, r)

    def probs(self, s):
        return jnp.exp2(s).astype(_BF16)

    def weigh(self, j, pb):
        blocks = [j]
        if self.merge_pv:
            self._pbs.append(pb)
            if j + 1 < len(self.k_parts):
                return
            blocks, pb = range(len(self.k_parts)), jnp.concatenate(self._pbs, axis=0)
        for gi, (ls, vs) in enumerate(self.v_groups):
            v = jnp.concatenate([vs[jj]() for jj in blocks], axis=1)
            v1 = jnp.concatenate([v, jnp.ones((_ONES_ROWS, v.shape[1]), v.dtype)], axis=0)
            d = jnp.dot(v1, pb[:, ls], preferred_element_type=_F32)
            self.accs[gi] = d if self.accs[gi] is None else self.accs[gi] + d

    def finish(self, sink_term=None):
        if self.sink is not None and sink_term is None:
            sink_term = self.sink()[1:2, :]
        outs = []
        for acc, (ls, _) in zip(self.accs, self.v_groups):
            l = acc[HEAD_DIM:HEAD_DIM + 1, :]
            if sink_term is not None:
                l = l + sink_term[:, ls]
            outs.append(acc[:HEAD_DIM, :] * (1.0 / l))
        self.done(outs)

    def run_exact(self, scr):
        m = None
        for j in range(len(self.k_parts)):
            s = self.scores(j)
            scr[self.rows[j]:self.rows[j + 1], :] = s
            mj = jnp.max(s, axis=0, keepdims=True)
            m = mj if m is None else jnp.maximum(m, mj)
        sink = None if self.sink is None else self.sink()[0:1, :]
        if sink is not None:
            m = jnp.maximum(m, sink)
        for j in range(len(self.k_parts)):
            self.weigh(j, self.probs(scr[self.rows[j]:self.rows[j + 1], :] - m))
        self.finish(None if sink is None else jnp.exp2(sink - m))


def _attn_kernel(exact_max, tps, q_ref, g_ref, kp_ref, kc_ref, kn_ref, vp_ref, vc_ref, vn_ref, km_ref, vm_ref,
                 bna_first_ref, bna_last_ref, bna_mid_ref, bwin_ref, sink_ref, bmem_ref, o_ref, mix_scr, *score_scr):
    n = pl.program_id(0)
    n_steps = pl.num_programs(0)
    wb = WINDOW
    full = slice(0, TILE)
    scr_na, scr_win = score_scr if exact_max else (None, None)

    def k_rows(tile, r0, r, lanes):
        if tile < 0:
            return lambda: kp_ref[r0:r0 + r, lanes]
        if tile >= tps:
            return lambda: kn_ref[r0:r0 + r, lanes]
        return lambda: kc_ref[tile * TILE + r0:tile * TILE + r0 + r, lanes]

    def v_cols(tile, rows, c0, c):
        if tile < 0:
            return lambda: vp_ref[rows, c0:c0 + c]
        if tile >= tps:
            return lambda: vn_ref[rows, c0:c0 + c]
        return lambda: vc_ref[tile, rows, c0:c0 + c]

    def gate_store(i, r0, lanes, o):
        gate = g_ref[i, r0:r0 + HEAD_DIM, lanes].astype(_F32)
        mix_scr[i, r0:r0 + HEAD_DIM, lanes] = (o * gate).astype(_BF16)

    def na_unit(i, h):
        hrows = slice(h * HEAD_DIM, (h + 1) * HEAD_DIM)
        lanes = slice(128 * (h // 2), 128 * (h // 2) + 128)
        bias_ref = bna_first_ref if i == 0 else bna_last_ref if i == tps - 1 else bna_mid_ref
        next_rows = TILE if i == 0 else (NA_ROWS - 1 - TILE_ROWS) * GRID_W
        spans = [(-1, TILE), (0, TILE), (1, next_rows)]
        return _Unit(
            [k_rows(i + d, 0, r, lanes) for d, r in spans],
            lambda: _pad_head(q_ref[i, hrows, :], h % 2),
            lambda r0, r: bias_ref[h, r0:r0 + r, :],
            [(full, [v_cols(i + d, hrows, 0, r) for d, r in spans])], None,
            lambda outs: gate_store(i, h * HEAD_DIM, full, outs[0]))

    def win_unit(i, blk):
        qlanes = slice(blk * wb, (blk + 1) * wb)
        per_tile = TILE // wb
        src = [divmod(i * per_tile + blk + d, per_tile) for d in (-1, 0, 1)]
        src = [(t, b * wb) for t, b in src]
        if i == 0 and blk == 0:
            variant = jnp.where(n == 0, 0, 1)
        elif i == tps - 1 and blk == per_tile - 1:
            variant = jnp.where(n == n_steps - 1, 2, 1)
        else:
            variant = 1

        def rhs():
            rows = []
            for kv in range(N_KV_WIN):
                q_cat = jnp.concatenate(
                    [q_ref[i, D_NA + (kv * WIN_G + g) * HEAD_DIM:D_NA + (kv * WIN_G + g + 1) * HEAD_DIM, qlanes]
                     for g in range(WIN_G)], axis=1)
                z = jnp.zeros_like(q_cat)
                rows.append(jnp.concatenate([z, q_cat] if kv else [q_cat, z], axis=1))
            return jnp.concatenate(rows, axis=0)

        def done(outs):
            for kv, o in enumerate(outs):
                for g in range(WIN_G):
                    gate_store(i, D_NA + (kv * WIN_G + g) * HEAD_DIM, qlanes, o[:, g * wb:(g + 1) * wb])

        groups = []
        for kv in range(N_KV_WIN):
            vrows = slice(D_NA + kv * HEAD_DIM, D_NA + (kv + 1) * HEAD_DIM)
            groups.append((slice(kv * WIN_G * wb, (kv + 1) * WIN_G * wb), [v_cols(t, vrows, r0, wb) for t, r0 in src]))
        return _Unit(
            [k_rows(t, r0, wb, slice(D_NA, D_NA + D_KV_WIN)) for t, r0 in src],
            rhs, lambda r0, r: bwin_ref[variant, r0:r0 + r, :], groups, lambda: sink_ref[...], done, merge_pv=True)

    def mem_unit(i, h):
        r0 = D_NA + D_WIN + h * HEAD_DIM
        lanes = slice(128 * (h // 2), 128 * (h // 2) + 128)
        return _Unit(
            [lambda: km_ref[:, lanes]],
            lambda: _pad_head(q_ref[i, r0:r0 + HEAD_DIM, :], h % 2),
            lambda r0_, r: bmem_ref[h],
            [(full, [lambda: vm_ref[h * HEAD_DIM:(h + 1) * HEAD_DIM, :]])], None,
            lambda outs: gate_store(i, r0, full, outs[0]))

    def emit_chunk(i, c):
        cs = slice(c * 256, (c + 1) * 256)
        o_ref[i * TILE:(i + 1) * TILE, cs] = mix_scr[i, cs, :].T

    units, ready = [], {}
    for i in range(tps):
        base = len(units)
        units += ([mem_unit(i, h) for h in range(N_HEADS_MEM)] + [win_unit(i, b) for b in range(TILE // wb)]
                  + [na_unit(i, h) for h in range(N_HEADS_NA)])
        for after, chunks in {3: [3], 5: [2], 9: [0], 11: [1]}.items():
            ready[base + after] = [functools.partial(emit_chunk, i, c) for c in chunks]
    pending = []

    if exact_max:
        for ui, unit in enumerate(units):
            unit.run_exact(scr_win if unit.sink is not None else scr_na)
            for emit in ready.get(ui, []):
                emit()
        return

    blocks = [(ui, j) for ui, unit in enumerate(units) for j in range(len(unit.k_parts))]
    s_q, p_q = {}, {}
    for t in range(len(blocks) + PV_LAG):
        if t < len(blocks):
            ui, j = blocks[t]
            s_q[t] = units[ui].scores(j)
        if pending:
            pending.pop(0)()
        if 0 <= t - EXP_LAG < len(blocks):
            ui, j = blocks[t - EXP_LAG]
            p_q[t - EXP_LAG] = units[ui].probs(s_q.pop(t - EXP_LAG))
        if 0 <= t - PV_LAG < len(blocks):
            ui, j = blocks[t - PV_LAG]
            units[ui].weigh(j, p_q.pop(t - PV_LAG))
            if j == len(units[ui].k_parts) - 1:
                units[ui].finish()
                pending.extend(ready.get(ui, []))
    while pending:
        pending.pop(0)()


def _attn_call(layer, exact_max, q, g, k, v, km, vm, bna, bwin, sink_rows, bmem):
    b, t, _ = k.shape
    nt = t // TILE
    tps = next(c for c in (TILES_PER_STEP, 3, 2) if nt % c == 0)
    steps = nt // tps
    last = steps - 1
    step_tiles = lambda rows: pl.BlockSpec((None, tps, rows, TILE), lambda n, i: (i, n, 0, 0))
    one_tile = lambda rows, tile_of: pl.BlockSpec((None, None, rows, TILE), lambda n, i: (i, tile_of(n), 0, 0))
    prev_tile = lambda n: jnp.maximum(n * tps - 1, 0)
    next_tile = lambda n: jnp.minimum((n + 1) * tps, nt - 1)
    const = lambda shape, index: pl.BlockSpec(shape, index, pipeline_mode=pl.Buffered(1))
    na_table = lambda variant_of: const((None, None, N_HEADS_NA, 3 * TILE, TILE),
                                        lambda n, i: (layer, variant_of(n), 0, 0, 0))
    scratch = [pltpu.VMEM((tps, D_MIX, TILE), _BF16)]
    if exact_max:
        scratch += [pltpu.VMEM((3 * TILE, TILE), _F32), pltpu.VMEM((3 * WINDOW, WIN_LANES), _F32)]
    return pl.pallas_call(
        functools.partial(_attn_kernel, exact_max, tps),
        grid=(steps, b),
        in_specs=[
            step_tiles(D_MIX),
            step_tiles(D_MIX),
            pl.BlockSpec((None, TILE, D_K), lambda n, i: (i, prev_tile(n), 0)),
            pl.BlockSpec((None, tps * TILE, D_K), lambda n, i: (i, n, 0)),
            pl.BlockSpec((None, TILE, D_K), lambda n, i: (i, next_tile(n), 0)),
            one_tile(D_K, prev_tile),
            step_tiles(D_K),
            one_tile(D_K, next_tile),
            pl.BlockSpec((None, None, N_MEM, D_MEM), lambda n, i: (layer, i, 0, 0)),
            pl.BlockSpec((None, None, D_MEM, N_MEM), lambda n, i: (layer, i, 0, 0)),
            na_table(lambda n: jnp.where(n == 0, 0, 1)),
            na_table(lambda n: jnp.where(n == last, 2, 1)),
            na_table(lambda n: 1),
            const((None, 3, 3 * WINDOW, WIN_LANES), lambda n, i: (layer, 0, 0, 0)),
            pl.BlockSpec((None, 2, WIN_LANES), lambda n, i: (layer, 0, 0)),
            pl.BlockSpec((None, N_HEADS_MEM, 1, TILE), lambda n, i: (layer, 0, 0, 0)),
        ],
        out_specs=pl.BlockSpec((None, tps * TILE, D_MIX), lambda n, i: (i, n, 0)),
        out_shape=jax.ShapeDtypeStruct((b, t, D_MIX), _BF16),
        scratch_shapes=scratch,
        compiler_params=pltpu.CompilerParams(
            dimension_semantics=("arbitrary", "arbitrary"), vmem_limit_bytes=VMEM_LIMIT_BYTES),
        name="attn_exact" if exact_max else "attn",
    )(q, g, k, k, k, v, v, v, km, vm, bna, bna, bna, bwin, sink_rows, bmem)


def _na_index_tables():
    n_tiles = 4
    rows = n_tiles * TILE_ROWS
    rel_rows, row_ok = [], []
    for n in (0, 1, n_tiles - 1):
        qr = n * TILE_ROWS + np.arange(TILE_ROWS)
        kr = (n - 1) * TILE_ROWS + np.arange(3 * TILE_ROWS)
        r0 = np.clip(qr - NA_ROWS // 2, 0, rows - NA_ROWS)
        ok = ((kr[:, None] >= r0[None, :]) & (kr[:, None] < r0[None, :] + NA_ROWS)
              & (kr[:, None] >= 0) & (kr[:, None] < rows))
        rel_rows.append(np.clip(kr[:, None] - qr[None, :] + NA_ROWS - 1, 0, 2 * NA_ROWS - 2))
        row_ok.append(ok)
    qc = np.arange(GRID_W)
    kc = np.arange(GRID_W)
    cs = np.clip(qc - NA_COLS // 2, 0, GRID_W - NA_COLS)
    col_ok = (kc[:, None] >= cs[None, :]) & (kc[:, None] < cs[None, :] + NA_COLS)
    rel_col = np.clip(kc[:, None] - qc[None, :] + NA_COLS - 1, 0, 2 * NA_COLS - 2)
    return np.stack(rel_rows), np.stack(row_ok), rel_col, col_ok


def _na_table_kernel(rel_rows, row_ok, t1_ref, out_ref):
    left_half = lax.broadcasted_iota(jnp.int32, (GRID_W, 2 * GRID_W), 1) < GRID_W
    masked = jnp.full((GRID_W, 2 * GRID_W), NEG_INF, _F32)
    for v in range(3):
        for kr in range(3 * TILE_ROWS):
            for pair in range(TILE_ROWS // 2):
                halves = [t1_ref[int(rel_rows[v, kr, qr])] if row_ok[v, kr, qr] else masked
                          for qr in (2 * pair, 2 * pair + 1)]
                out_ref[v, kr * GRID_W:(kr + 1) * GRID_W, pair * 2 * GRID_W:(pair + 1) * 2 * GRID_W] = (
                    jnp.where(left_half, halves[0], halves[1]))


def _na_bias(rpb2, shift):
    depth = rpb2.shape[0]
    rel_rows, row_ok, rel_col, col_ok = _na_index_tables()
    onehot = (rel_col.reshape(-1)[None, :] == np.arange(2 * NA_COLS - 1)[:, None]).astype(np.float32)
    t1 = jnp.einsum("lhab,bn->lhan", rpb2 - shift[:, :, None, None], jnp.asarray(onehot),
                    precision=lax.Precision.HIGHEST)
    t1 = jnp.where(jnp.asarray(col_ok.reshape(-1)), t1, NEG_INF).reshape(depth, N_HEADS_NA, 2 * NA_ROWS - 1, GRID_W, GRID_W)
    t1 = jnp.concatenate([t1, t1], axis=-1)
    return pl.pallas_call(
        functools.partial(_na_table_kernel, rel_rows, row_ok),
        grid=(depth, N_HEADS_NA),
        in_specs=[pl.BlockSpec((None, None, 2 * NA_ROWS - 1, GRID_W, 2 * GRID_W), lambda l, h: (l, h, 0, 0, 0))],
        out_specs=pl.BlockSpec((None, 3, None, 3 * TILE, TILE), lambda l, h: (l, 0, h, 0, 0)),
        out_shape=jax.ShapeDtypeStruct((depth, 3, N_HEADS_NA, 3 * TILE, TILE), _F32),
        compiler_params=pltpu.CompilerParams(
            dimension_semantics=("arbitrary", "arbitrary"), vmem_limit_bytes=VMEM_LIMIT_BYTES),
        name="na_table",
    )(t1)


def _win_bias(shift):
    i = np.arange(WINDOW)
    w = np.arange(3 * WINDOW)
    dist = (i[None, :] - w[:, None] + WINDOW).astype(np.float32)
    band = np.abs(dist) <= WINDOW
    edge = np.stack([w >= WINDOW, w >= 0, w < 2 * WINDOW])
    valid = band[None] & edge[:, :, None]
    slopes = 2.0 ** (-8.0 * jnp.arange(1, N_HEADS_WIN + 1, dtype=_F32) / N_HEADS_WIN)
    ali = -slopes[None, :, None] * jnp.abs(jnp.asarray(dist))[:, None, :] * LOG2E
    shifted = ali[None] - shift[:, None, :, None]
    bias = jnp.where(jnp.asarray(valid)[None, :, :, None, :], shifted[:, None], NEG_INF)
    return bias.reshape(shift.shape[0], 3, 3 * WINDOW, WIN_LANES).astype(_F32)


def _prep_weights(norm_g, w_in, q_norm_g, k_norm_g, rpb, sink, mem_norm_g, w_mem_kv, w_out):
    depth = w_in.shape[0]
    sizes = [D_NA, D_NA, D_NA, D_NA, D_WIN, D_KV_WIN, D_KV_WIN, D_WIN, D_MEM, D_MEM]
    offs = np.concatenate([[0], np.cumsum(sizes)])
    seg = lambda i: w_in[:, :, offs[i]:offs[i + 1]]
    na_q, na_k, na_v, na_g, wq, wk, wv, wg, mq, mg = [seg(i) for i in range(10)]
    w_perm = jnp.concatenate([na_q, wq, mq, na_k, wk, na_v, wv, na_g, wg, mg], axis=-1)
    wt = jnp.swapaxes(w_perm, 1, 2).astype(_BF16)
    q_scale = HEAD_DIM ** -0.5 * LOG2E
    heads = (N_HEADS_NA, N_HEADS_WIN, N_HEADS_MEM)
    gq = jnp.concatenate([jnp.tile(q_norm_g[:, i], (1, n)) for i, n in enumerate(heads)], axis=-1) * q_scale
    gk = jnp.concatenate([jnp.tile(k_norm_g[:, 0], (1, N_HEADS_NA)), jnp.tile(k_norm_g[:, 1], (1, N_KV_WIN))], axis=-1)
    gkm = jnp.tile(k_norm_g[:, 2], (1, N_HEADS_MEM))
    sink2 = sink.astype(_F32) * LOG2E
    rpb2 = rpb.astype(_F32) * LOG2E

    q_bound = HEAD_DIM ** 0.5 * jnp.max(jnp.abs(q_norm_g), axis=-1) * q_scale
    k_bound = HEAD_DIM ** 0.5 * jnp.max(jnp.abs(k_norm_g), axis=-1)
    qk = q_bound * k_bound
    rpb_max, rpb_min = jnp.max(rpb2, axis=(2, 3)), jnp.min(rpb2, axis=(2, 3))
    shift_na = qk[:, 0:1] + rpb_max
    shift_win = jnp.maximum(qk[:, 1:2], sink2)
    shift_mem = jnp.broadcast_to(qk[:, 2:3], (depth, N_HEADS_MEM))
    slack = jnp.max(2.0 * qk + jnp.stack([jnp.max(rpb_max - rpb_min, axis=1), jnp.zeros((depth,), _F32),
                                          jnp.zeros((depth,), _F32)], axis=1), axis=1)
    bound_ok = slack <= MAX_SHIFT_SLACK
    use = bound_ok[:, None].astype(_F32)
    shift_na, shift_win, shift_mem = shift_na * use, shift_win * use, shift_mem * use
    sink_logit = jnp.repeat(sink2 - shift_win, WINDOW, axis=-1)
    return dict(
        ng=norm_g.reshape(depth, 1, D_MODEL), wt=wt,
        gq=gq.reshape(depth, D_MIX, 1), gk=gk.reshape(depth, D_K, 1), gkm=gkm.reshape(depth, D_MEM, 1),
        bna=_na_bias(rpb2, shift_na), bwin=_win_bias(shift_win),
        sink_rows=jnp.stack([sink_logit, jnp.exp2(sink_logit)], axis=1),
        bmem=jnp.broadcast_to(-shift_mem[:, :, None, None], (depth, N_HEADS_MEM, 1, TILE)),
        mg=mem_norm_g.reshape(depth, 1, D_MODEL),
        wmt=jnp.swapaxes(w_mem_kv, 1, 2).astype(_BF16),
        wo=w_out.astype(_BF16),
        bound_ok=bound_ok,
    )


def _forward(xs, mems, norm_g, w_in, q_norm_g, k_norm_g, rpb, sink, mem_norm_g, w_mem_kv, w_out):
    depth = w_in.shape[0]
    w = _prep_weights(norm_g, w_in, q_norm_g, k_norm_g, rpb, sink, mem_norm_g, w_mem_kv, w_out)
    mem_kv = [_memkv_call(mem, w["mg"], w["wmt"], w["gkm"]) for mem in mems]
    xs = list(xs)
    mixes = [None] * len(xs)
    tables = (w["bna"], w["bwin"], w["sink_rows"], w["bmem"])
    for l in range(depth):
        for gi in range(len(xs)):
            outs = _proj_call(l, xs[gi], w, mixes[gi])
            if mixes[gi] is not None:
                xs[gi], outs = outs[0], outs[1:]
            q, k, v, g = outs
            mixes[gi] = lax.cond(w["bound_ok"][l],
                                 functools.partial(_attn_call, l, False),
                                 functools.partial(_attn_call, l, True),
                                 q, g, k, v, *mem_kv[gi], *tables)
    return tuple(_proj_call(None, x, w, mix)[0] for x, mix in zip(xs, mixes))


def kernel(x_prompt, x_sample, mem_prompt, mem_sample, norm_g, w_in, q_norm_g, k_norm_g, rpb, sink, mem_norm_g, w_mem_kv, w_out):
    return _forward((x_prompt, x_sample), (mem_prompt, mem_sample), norm_g, w_in, q_norm_g, k_norm_g, rpb, sink,
                    mem_norm_g, w_mem_kv, w_out)
```

```python
import functools

import numpy as np
import jax
import jax.numpy as jnp
from jax import lax
from jax.experimental import pallas as pl
from jax.experimental.pallas import tpu as pltpu

D_MODEL = 1024
HEAD_DIM = 64
N_HEADS_NA = 6
N_HEADS_WIN = 6
N_KV_WIN = 2
N_HEADS_MEM = 4
D_NA = N_HEADS_NA * HEAD_DIM
D_WIN = N_HEADS_WIN * HEAD_DIM
D_KV_WIN = N_KV_WIN * HEAD_DIM
D_MEM = N_HEADS_MEM * HEAD_DIM
D_MIX = D_NA + D_WIN + D_MEM
D_K = D_NA + D_KV_WIN
D_PROJ = D_MIX + 2 * D_K + D_MIX
N_MEM = 256
GRID_W = 64
NA_ROWS = 8
NA_COLS = 16
WINDOW = 128
RMS_EPS = 1e-6
NEG_INF = -1e30
LOG2E = 1.4426950408889634
MAX_SHIFT_SLACK = 64.0

TILE = 256
TILE_ROWS = TILE // GRID_W
PROJ_LAG = 1
EXP_LAG = 1
PV_LAG = 6
TILES_PER_STEP = 4
WIN_G = N_HEADS_WIN // N_KV_WIN
WIN_LANES = N_KV_WIN * WIN_G * WINDOW
VMEM_LIMIT_BYTES = 56 * 1024 * 1024

_BF16 = jnp.bfloat16
_F32 = jnp.float32
_NT = (((1,), (1,)), ((), ()))
_ONES_ROWS = 16


def _rms_rows(x, gain):
    ms = jnp.mean(x * x, axis=-1, keepdims=True)
    return x * lax.rsqrt(ms + RMS_EPS) * gain


def _head_norm_cols(p, gain):
    ms = jnp.mean(p * p, axis=0, keepdims=True)
    return p * lax.rsqrt(ms + RMS_EPS) * gain


def _proj_kernel(has_out, has_in, *refs):
    refs = list(refs)
    x_ref = refs.pop(0)
    mix_ref, wo_ref = (refs.pop(0), refs.pop(0)) if has_out else (None, None)
    ng_ref, wt_ref, gq_ref, gk_ref = [refs.pop(0) for _ in range(4)] if has_in else [None] * 4
    xo_ref = refs.pop(0) if has_out else None
    q_out, k_out, v_out, g_out, xn_scr = refs if has_in else [None] * 5
    ta = x_ref.shape[0]
    n_tiles = ta // TILE
    chunk = 256
    x_src = xo_ref if has_out else x_ref

    def normalise(j):
        rows = slice(j * TILE, (j + 1) * TILE)
        xn_scr[:, rows] = _rms_rows(x_src[rows, :], ng_ref[...]).astype(_BF16).T

    def queries(c0, j, p):
        for hh in range(chunk // HEAD_DIM):
            r0 = c0 + hh * HEAD_DIM
            qh = _head_norm_cols(p[hh * HEAD_DIM:(hh + 1) * HEAD_DIM, :], gq_ref[r0:r0 + HEAD_DIM, :])
            q_out[j, r0:r0 + HEAD_DIM, :] = qh.astype(_BF16)

    def keys(c0, j, p):
        kn = jnp.concatenate(
            [_head_norm_cols(p[hh * HEAD_DIM:(hh + 1) * HEAD_DIM, :], gk_ref[c0 + hh * HEAD_DIM:c0 + (hh + 1) * HEAD_DIM, :])
             for hh in range(chunk // HEAD_DIM)], axis=0)
        k_out[j * TILE:(j + 1) * TILE, c0:c0 + chunk] = kn.T.astype(_BF16)

    def values(c0, j, p):
        v_out[j, c0:c0 + chunk, :] = p.astype(_BF16)

    def gates(c0, j, p):
        g_out[j, c0:c0 + chunk, :] = (p * jax.nn.sigmoid(p)).astype(_BF16)

    def residual(c0, j, y):
        rows = slice(j * TILE, (j + 1) * TILE)
        xo_ref[rows, c0:c0 + chunk] = x_ref[rows, c0:c0 + chunk] + y

    def in_items(j):
        sections = [(0, D_MIX, queries), (D_MIX, D_K, keys), (D_MIX + 2 * D_K, D_MIX, gates), (D_MIX + D_K, D_K, values)]
        return [(functools.partial(
                    lambda r0: jnp.dot(wt_ref[r0:r0 + chunk, :], xn_scr[:, j * TILE:(j + 1) * TILE],
                                       preferred_element_type=_F32), base + c0),
                 functools.partial(epilogue, c0, j))
                for base, rows, epilogue in sections for c0 in range(0, rows, chunk)]

    def out_items(j):
        return [(functools.partial(
                    lambda c0: jnp.dot(mix_ref[j * TILE:(j + 1) * TILE, :], wo_ref[:, c0:c0 + chunk],
                                       preferred_element_type=_F32), c0),
                 functools.partial(residual, c0, j))
                for c0 in range(0, D_MODEL, chunk)]

    order = []
    if has_out:
        order += out_items(0)
    if has_in:
        order.append(functools.partial(normalise, 0))
        for j in range(n_tiles):
            items = in_items(j)
            third = len(items) // 3
            order += items[:third]
            if j + 1 < n_tiles:
                if has_out:
                    order += out_items(j + 1)
                order += items[third:2 * third]
                order.append(functools.partial(normalise, j + 1))
                order += items[2 * third:]
            else:
                order += items[third:]
    else:
        for j in range(1, n_tiles):
            order += out_items(j)

    pending = []
    for entry in order:
        if callable(entry):
            while pending:
                epilogue, result = pending.pop(0)
                epilogue(result)
            entry()
            continue
        matmul, epilogue = entry
        pending.append((epilogue, matmul()))
        while len(pending) > PROJ_LAG:
            epilogue, result = pending.pop(0)
            epilogue(result)
    while pending:
        epilogue, result = pending.pop(0)
        epilogue(result)


def _proj_call(layer, x, w, mix=None):
    b, t, _ = x.shape
    has_out, has_in = mix is not None, layer is not None
    ta = next(c for c in (1024, 512, TILE) if t % c == 0)
    nt = t // TILE
    tpb = ta // TILE
    tokens = lambda cols: pl.BlockSpec((None, ta, cols), lambda i, j: (i, j, 0))
    tiles = lambda rows: pl.BlockSpec((None, tpb, rows, TILE), lambda i, j: (i, j, 0, 0))
    per_layer = lambda l, *shape: pl.BlockSpec((None,) + shape, lambda i, j: (l,) + (0,) * len(shape),
                                               pipeline_mode=pl.Buffered(1))
    operands, in_specs, out_specs, out_shape, scratch = [x], [tokens(D_MODEL)], [], [], []
    if has_out:
        prev = (w["wo"].shape[0] - 1) if layer is None else layer - 1
        operands += [mix, w["wo"]]
        in_specs += [tokens(D_MIX), per_layer(prev, D_MIX, D_MODEL)]
        out_specs.append(tokens(D_MODEL))
        out_shape.append(jax.ShapeDtypeStruct(x.shape, x.dtype))
    if has_in:
        operands += [w["ng"], w["wt"], w["gq"], w["gk"]]
        in_specs += [per_layer(layer, 1, D_MODEL), per_layer(layer, D_PROJ, D_MODEL), per_layer(layer, D_MIX, 1),
                     per_layer(layer, D_K, 1)]
        out_specs += [tiles(D_MIX), tokens(D_K), tiles(D_K), tiles(D_MIX)]
        out_shape += [
            jax.ShapeDtypeStruct((b, nt, D_MIX, TILE), _BF16),
            jax.ShapeDtypeStruct((b, t, D_K), _BF16),
            jax.ShapeDtypeStruct((b, nt, D_K, TILE), _BF16),
            jax.ShapeDtypeStruct((b, nt, D_MIX, TILE), _BF16),
        ]
        scratch.append(pltpu.VMEM((D_MODEL, ta), _BF16))
    return pl.pallas_call(
        functools.partial(_proj_kernel, has_out, has_in),
        grid=(b, t // ta),
        in_specs=in_specs, out_specs=out_specs, out_shape=out_shape, scratch_shapes=scratch,
        compiler_params=pltpu.CompilerParams(
            dimension_semantics=("arbitrary", "arbitrary"), vmem_limit_bytes=VMEM_LIMIT_BYTES),
        name="proj",
    )(*operands)


def _memkv_kernel(mem_ref, g_ref, wt_ref, gk_ref, km_out, vm_out):
    x = mem_ref[...]
    xr = x * lax.rsqrt(jnp.mean(x * x, axis=-1, keepdims=True) + RMS_EPS)
    for l in range(g_ref.shape[0]):
        xn = (xr * g_ref[l]).astype(_BF16)
        p = lax.dot_general(wt_ref[l], xn, _NT, preferred_element_type=_F32)
        kn = jnp.concatenate(
            [_head_norm_cols(p[h * HEAD_DIM:(h + 1) * HEAD_DIM, :], gk_ref[l, h * HEAD_DIM:(h + 1) * HEAD_DIM, :])
             for h in range(N_HEADS_MEM)], axis=0)
        km_out[l] = kn.T.astype(_BF16)
        vm_out[l] = p[D_MEM:, :].astype(_BF16)


def _memkv_call(mem, g, wt, gk):
    bm = mem.shape[0]
    depth = g.shape[0]
    whole = lambda *shape: pl.BlockSpec((depth,) + shape, lambda i: (0,) * (1 + len(shape)),
                                        pipeline_mode=pl.Buffered(1))
    return pl.pallas_call(
        _memkv_kernel,
        grid=(bm,),
        in_specs=[
            pl.BlockSpec((None, N_MEM, D_MODEL), lambda i: (i, 0, 0)),
            whole(1, D_MODEL),
            whole(2 * D_MEM, D_MODEL),
            whole(D_MEM, 1),
        ],
        out_specs=[
            pl.BlockSpec((depth, None, N_MEM, D_MEM), lambda i: (0, i, 0, 0)),
            pl.BlockSpec((depth, None, D_MEM, N_MEM), lambda i: (0, i, 0, 0)),
        ],
        out_shape=[
            jax.ShapeDtypeStruct((depth, bm, N_MEM, D_MEM), _BF16),
            jax.ShapeDtypeStruct((depth, bm, D_MEM, N_MEM), _BF16),
        ],
        compiler_params=pltpu.CompilerParams(
            dimension_semantics=("arbitrary",), vmem_limit_bytes=VMEM_LIMIT_BYTES),
        name="memkv",
    )(mem, g, wt, gk)


def _pad_head(q, odd):
    z = jnp.zeros_like(q)
    return jnp.concatenate([z, q] if odd else [q, z], axis=0)


class _Unit:
    def __init__(self, k_parts, rhs, bias_of, v_groups, sink, done, merge_pv=False):
        self.k_parts, self.rhs, self.bias_of = k_parts, rhs, bias_of
        self.v_groups, self.sink, self.done = v_groups, sink, done
        self.merge_pv = merge_pv
        self.rows = [0]
        self._rhs = None
        self._pbs = []
        self.accs = [None] * len(v_groups)

    def scores(self, j):
        if self._rhs is None:
            self._rhs = self.rhs()
        k_blk = self.k_parts[j]()
        r0, r = self.rows[j], k_blk.shape[0]
        if len(self.rows) == j + 1:
            self.rows.append(r0 + r)
        return jnp.dot(k_blk, self._rhs, preferred_element_type=_F32) + self.bias_of(r0, r)

    def probs(self, s):
        return jnp.exp2(s).astype(_BF16)

    def weigh(self, j, pb):
        blocks = [j]
        if self.merge_pv:
            self._pbs.append(pb)
            if j + 1 < len(self.k_parts):
                return
            blocks, pb = range(len(self.k_parts)), jnp.concatenate(self._pbs, axis=0)
        for gi, (ls, vs) in enumerate(self.v_groups):
            v = jnp.concatenate([vs[jj]() for jj in blocks], axis=1)
            v1 = jnp.concatenate([v, jnp.ones((_ONES_ROWS, v.shape[1]), v.dtype)], axis=0)
            d = jnp.dot(v1, pb[:, ls], preferred_element_type=_F32)
            self.accs[gi] = d if self.accs[gi] is None else self.accs[gi] + d

    def finish(self, sink_term=None):
        if self.sink is not None and sink_term is None:
            sink_term = self.sink()[1:2, :]
        outs = []
        for acc, (ls, _) in zip(self.accs, self.v_groups):
            l = acc[HEAD_DIM:HEAD_DIM + 1, :]
            if sink_term is not None:
                l = l + sink_term[:, ls]
            outs.append(acc[:HEAD_DIM, :] * (1.0 / l))
        self.done(outs)

    def run_exact(self, scr):
        m = None
        for j in range(len(self.k_parts)):
            s = self.scores(j)
            scr[self.rows[j]:self.rows[j + 1], :] = s
            mj = jnp.max(s, axis=0, keepdims=True)
            m = mj if m is None else jnp.maximum(m, mj)
        sink = None if self.sink is None else self.sink()[0:1, :]
        if sink is not None:
            m = jnp.maximum(m, sink)
        for j in range(len(self.k_parts)):
            self.weigh(j, self.probs(scr[self.rows[j]:self.rows[j + 1], :] - m))
        self.finish(None if sink is None else jnp.exp2(sink - m))


def _attn_kernel(exact_max, tps, q_ref, g_ref, kp_ref, kc_ref, kn_ref, vp_ref, vc_ref, vn_ref, km_ref, vm_ref,
                 bna_first_ref, bna_last_ref, bna_mid_ref, bwin_ref, sink_ref, bmem_ref, o_ref, mix_scr, *score_scr):
    n = pl.program_id(0)
    n_steps = pl.num_programs(0)
    wb = WINDOW
    full = slice(0, TILE)
    scr_na, scr_win = score_scr if exact_max else (None, None)

    def k_rows(tile, r0, r, lanes):
        if tile < 0:
            return lambda: kp_ref[r0:r0 + r, lanes]
        if tile >= tps:
            return lambda: kn_ref[r0:r0 + r, lanes]
        return lambda: kc_ref[tile * TILE + r0:tile * TILE + r0 + r, lanes]

    def v_cols(tile, rows, c0, c):
        if tile < 0:
            return lambda: vp_ref[rows, c0:c0 + c]
        if tile >= tps:
            return lambda: vn_ref[rows, c0:c0 + c]
        return lambda: vc_ref[tile, rows, c0:c0 + c]

    def gate_store(i, r0, lanes, o):
        gate = g_ref[i, r0:r0 + HEAD_DIM, lanes].astype(_F32)
        mix_scr[i, r0:r0 + HEAD_DIM, lanes] = (o * gate).astype(_BF16)

    def na_unit(i, h):
        hrows = slice(h * HEAD_DIM, (h + 1) * HEAD_DIM)
        lanes = slice(128 * (h // 2), 128 * (h // 2) + 128)
        bias_ref = bna_first_ref if i == 0 else bna_last_ref if i == tps - 1 else bna_mid_ref
        next_rows = TILE if i == 0 else (NA_ROWS - 1 - TILE_ROWS) * GRID_W
        spans = [(-1, TILE), (0, TILE), (1, next_rows)]
        return _Unit(
            [k_rows(i + d, 0, r, lanes) for d, r in spans],
            lambda: _pad_head(q_ref[i, hrows, :], h % 2),
            lambda r0, r: bias_ref[h, r0:r0 + r, :],
            [(full, [v_cols(i + d, hrows, 0, r) for d, r in spans])], None,
            lambda outs: gate_store(i, h * HEAD_DIM, full, outs[0]))

    def win_unit(i, blk):
        qlanes = slice(blk * wb, (blk + 1) * wb)
        per_tile = TILE // wb
        src = [divmod(i * per_tile + blk + d, per_tile) for d in (-1, 0, 1)]
        src = [(t, b * wb) for t, b in src]
        if i == 0 and blk == 0:
            variant = jnp.where(n == 0, 0, 1)
        elif i == tps - 1 and blk == per_tile - 1:
            variant = jnp.where(n == n_steps - 1, 2, 1)
        else:
            variant = 1

        def rhs():
            rows = []
            for kv in range(N_KV_WIN):
                q_cat = jnp.concatenate(
                    [q_ref[i, D_NA + (kv * WIN_G + g) * HEAD_DIM:D_NA + (kv * WIN_G + g + 1) * HEAD_DIM, qlanes]
                     for g in range(WIN_G)], axis=1)
                z = jnp.zeros_like(q_cat)
                rows.append(jnp.concatenate([z, q_cat] if kv else [q_cat, z], axis=1))
            return jnp.concatenate(rows, axis=0)

        def done(outs):
            for kv, o in enumerate(outs):
                for g in range(WIN_G):
                    gate_store(i, D_NA + (kv * WIN_G + g) * HEAD_DIM, qlanes, o[:, g * wb:(g + 1) * wb])

        groups = []
        for kv in range(N_KV_WIN):
            vrows = slice(D_NA + kv * HEAD_DIM, D_NA + (kv + 1) * HEAD_DIM)
            groups.append((slice(kv * WIN_G * wb, (kv + 1) * WIN_G * wb), [v_cols(t, vrows, r0, wb) for t, r0 in src]))
        return _Unit(
            [k_rows(t, r0, wb, slice(D_NA, D_NA + D_KV_WIN)) for t, r0 in src],
            rhs, lambda r0, r: bwin_ref[variant, r0:r0 + r, :], groups, lambda: sink_ref[...], done, merge_pv=True)

    def mem_unit(i, h):
        r0 = D_NA + D_WIN + h * HEAD_DIM
        lanes = slice(128 * (h // 2), 128 * (h // 2) + 128)
        return _Unit(
            [lambda: km_ref[:, lanes]],
            lambda: _pad_head(q_ref[i, r0:r0 + HEAD_DIM, :], h % 2),
            lambda r0_, r: bmem_ref[h],
            [(full, [lambda: vm_ref[h * HEAD_DIM:(h + 1) * HEAD_DIM, :]])], None,
            lambda outs: gate_store(i, r0, full, outs[0]))

    def emit_chunk(i, c):
        cs = slice(c * 256, (c + 1) * 256)
        o_ref[i * TILE:(i + 1) * TILE, cs] = mix_scr[i, cs, :].T

    units, ready = [], {}
    for i in range(tps):
        base = len(units)
        units += ([mem_unit(i, h) for h in range(N_HEADS_MEM)] + [win_unit(i, b) for b in range(TILE // wb)]
                  + [na_unit(i, h) for h in range(N_HEADS_NA)])
        for after, chunks in {3: [3], 5: [2], 9: [0], 11: [1]}.items():
            ready[base + after] = [functools.partial(emit_chunk, i, c) for c in chunks]
    pending = []

    if exact_max:
        for ui, unit in enumerate(units):
            unit.run_exact(scr_win if unit.sink is not None else scr_na)
            for emit in ready.get(ui, []):
                emit()
        return

    blocks = [(ui, j) for ui, unit in enumerate(units) for j in range(len(unit.k_parts))]
    s_q, p_q = {}, {}
    for t in range(len(blocks) + PV_LAG):
        if t < len(blocks):
            ui, j = blocks[t]
            s_q[t] = units[ui].scores(j)
        if pending:
            pending.pop(0)()
        if 0 <= t - EXP_LAG < len(blocks):
            ui, j = blocks[t - EXP_LAG]
            p_q[t - EXP_LAG] = units[ui].probs(s_q.pop(t - EXP_LAG))
        if 0 <= t - PV_LAG < len(blocks):
            ui, j = blocks[t - PV_LAG]
            units[ui].weigh(j, p_q.pop(t - PV_LAG))
            if j == len(units[ui].k_parts) - 1:
                units[ui].finish()
                pending.extend(ready.get(ui, []))
    while pending:
        pending.pop(0)()


def _attn_call(layer, exact_max, q, g, k, v, km, vm, bna, bwin, sink_rows, bmem):
    b, t, _ = k.shape
    nt = t // TILE
    tps = next(c for c in (TILES_PER_STEP, 3, 2) if nt % c == 0)
    steps = nt // tps
    last = steps - 1
    step_tiles = lambda rows: pl.BlockSpec((None, tps, rows, TILE), lambda n, i: (i, n, 0, 0))
    one_tile = lambda rows, tile_of: pl.BlockSpec((None, None, rows, TILE), lambda n, i: (i, tile_of(n), 0, 0))
    prev_tile = lambda n: jnp.maximum(n * tps - 1, 0)
    next_tile = lambda n: jnp.minimum((n + 1) * tps, nt - 1)
    const = lambda shape, index: pl.BlockSpec(shape, index, pipeline_mode=pl.Buffered(1))
    na_table = lambda variant_of: const((None, None, N_HEADS_NA, 3 * TILE, TILE),
                                        lambda n, i: (layer, variant_of(n), 0, 0, 0))
    scratch = [pltpu.VMEM((tps, D_MIX, TILE), _BF16)]
    if exact_max:
        scratch += [pltpu.VMEM((3 * TILE, TILE), _F32), pltpu.VMEM((3 * WINDOW, WIN_LANES), _F32)]
    return pl.pallas_call(
        functools.partial(_attn_kernel, exact_max, tps),
        grid=(steps, b),
        in_specs=[
            step_tiles(D_MIX),
            step_tiles(D_MIX),
            pl.BlockSpec((None, TILE, D_K), lambda n, i: (i, prev_tile(n), 0)),
            pl.BlockSpec((None, tps * TILE, D_K), lambda n, i: (i, n, 0)),
            pl.BlockSpec((None, TILE, D_K), lambda n, i: (i, next_tile(n), 0)),
            one_tile(D_K, prev_tile),
            step_tiles(D_K),
            one_tile(D_K, next_tile),
            pl.BlockSpec((None, None, N_MEM, D_MEM), lambda n, i: (layer, i, 0, 0)),
            pl.BlockSpec((None, None, D_MEM, N_MEM), lambda n, i: (layer, i, 0, 0)),
            na_table(lambda n: jnp.where(n == 0, 0, 1)),
            na_table(lambda n: jnp.where(n == last, 2, 1)),
            na_table(lambda n: 1),
            const((None, 3, 3 * WINDOW, WIN_LANES), lambda n, i: (layer, 0, 0, 0)),
            pl.BlockSpec((None, 2, WIN_LANES), lambda n, i: (layer, 0, 0)),
            pl.BlockSpec((None, N_HEADS_MEM, 1, TILE), lambda n, i: (layer, 0, 0, 0)),
        ],
        out_specs=pl.BlockSpec((None, tps * TILE, D_MIX), lambda n, i: (i, n, 0)),
        out_shape=jax.ShapeDtypeStruct((b, t, D_MIX), _BF16),
        scratch_shapes=scratch,
        compiler_params=pltpu.CompilerParams(
            dimension_semantics=("arbitrary", "arbitrary"), vmem_limit_bytes=VMEM_LIMIT_BYTES),
        name="attn_exact" if exact_max else "attn",
    )(q, g, k, k, k, v, v, v, km, vm, bna, bna, bna, bwin, sink_rows, bmem)


def _na_index_tables():
    n_tiles = 4
    rows = n_tiles * TILE_ROWS
    rel_rows, row_ok = [], []
    for n in (0, 1, n_tiles - 1):
        qr = n * TILE_ROWS + np.arange(TILE_ROWS)
        kr = (n - 1) * TILE_ROWS + np.arange(3 * TILE_ROWS)
        r0 = np.clip(qr - NA_ROWS // 2, 0, rows - NA_ROWS)
        ok = ((kr[:, None] >= r0[None, :]) & (kr[:, None] < r0[None, :] + NA_ROWS)
              & (kr[:, None] >= 0) & (kr[:, None] < rows))
        rel_rows.append(np.clip(kr[:, None] - qr[None, :] + NA_ROWS - 1, 0, 2 * NA_ROWS - 2))
        row_ok.append(ok)
    qc = np.arange(GRID_W)
    kc = np.arange(GRID_W)
    cs = np.clip(qc - NA_COLS // 2, 0, GRID_W - NA_COLS)
    col_ok = (kc[:, None] >= cs[None, :]) & (kc[:, None] < cs[None, :] + NA_COLS)
    rel_col = np.clip(kc[:, None] - qc[None, :] + NA_COLS - 1, 0, 2 * NA_COLS - 2)
    return np.stack(rel_rows), np.stack(row_ok), rel_col, col_ok


def _na_table_kernel(rel_rows, row_ok, t1_ref, out_ref):
    left_half = lax.broadcasted_iota(jnp.int32, (GRID_W, 2 * GRID_W), 1) < GRID_W
    masked = jnp.full((GRID_W, 2 * GRID_W), NEG_INF, _F32)
    for v in range(3):
        for kr in range(3 * TILE_ROWS):
            for pair in range(TILE_ROWS // 2):
                halves = [t1_ref[int(rel_rows[v, kr, qr])] if row_ok[v, kr, qr] else masked
                          for qr in (2 * pair, 2 * pair + 1)]
                out_ref[v, kr * GRID_W:(kr + 1) * GRID_W, pair * 2 * GRID_W:(pair + 1) * 2 * GRID_W] = (
                    jnp.where(left_half, halves[0], halves[1]))


def _na_bias(rpb2, shift):
    depth = rpb2.shape[0]
    rel_rows, row_ok, rel_col, col_ok = _na_index_tables()
    onehot = (rel_col.reshape(-1)[None, :] == np.arange(2 * NA_COLS - 1)[:, None]).astype(np.float32)
    t1 = jnp.einsum("lhab,bn->lhan", rpb2 - shift[:, :, None, None], jnp.asarray(onehot),
                    precision=lax.Precision.HIGHEST)
    t1 = jnp.where(jnp.asarray(col_ok.reshape(-1)), t1, NEG_INF).reshape(depth, N_HEADS_NA, 2 * NA_ROWS - 1, GRID_W, GRID_W)
    t1 = jnp.concatenate([t1, t1], axis=-1)
    return pl.pallas_call(
        functools.partial(_na_table_kernel, rel_rows, row_ok),
        grid=(depth, N_HEADS_NA),
        in_specs=[pl.BlockSpec((None, None, 2 * NA_ROWS - 1, GRID_W, 2 * GRID_W), lambda l, h: (l, h, 0, 0, 0))],
        out_specs=pl.BlockSpec((None, 3, None, 3 * TILE, TILE), lambda l, h: (l, 0, h, 0, 0)),
        out_shape=jax.ShapeDtypeStruct((depth, 3, N_HEADS_NA, 3 * TILE, TILE), _F32),
        compiler_params=pltpu.CompilerParams(
            dimension_semantics=("arbitrary", "arbitrary"), vmem_limit_bytes=VMEM_LIMIT_BYTES),
        name="na_table",
    )(t1)


def _win_bias(shift):
    i = np.arange(WINDOW)
    w = np.arange(3 * WINDOW)
    dist = (i[None, :] - w[:, None] + WINDOW).astype(np.float32)
    band = np.abs(dist) <= WINDOW
    edge = np.stack([w >= WINDOW, w >= 0, w < 2 * WINDOW])
    valid = band[None] & edge[:, :, None]
    slopes = 2.0 ** (-8.0 * jnp.arange(1, N_HEADS_WIN + 1, dtype=_F32) / N_HEADS_WIN)
    ali = -slopes[None, :, None] * jnp.abs(jnp.asarray(dist))[:, None, :] * LOG2E
    shifted = ali[None] - shift[:, None, :, None]
    bias = jnp.where(jnp.asarray(valid)[None, :, :, None, :], shifted[:, None], NEG_INF)
    return bias.reshape(shift.shape[0], 3, 3 * WINDOW, WIN_LANES).astype(_F32)


def _prep_weights(norm_g, w_in, q_norm_g, k_norm_g, rpb, sink, mem_norm_g, w_mem_kv, w_out):
    depth = w_in.shape[0]
    sizes = [D_NA, D_NA, D_NA, D_NA, D_WIN, D_KV_WIN, D_KV_WIN, D_WIN, D_MEM, D_MEM]
    offs = np.concatenate([[0], np.cumsum(sizes)])
    seg = lambda i: w_in[:, :, offs[i]:offs[i + 1]]
    na_q, na_k, na_v, na_g, wq, wk, wv, wg, mq, mg = [seg(i) for i in range(10)]
    w_perm = jnp.concatenate([na_q, wq, mq, na_k, wk, na_v, wv, na_g, wg, mg], axis=-1)
    wt = jnp.swapaxes(w_perm, 1, 2).astype(_BF16)
    q_scale = HEAD_DIM ** -0.5 * LOG2E
    heads = (N_HEADS_NA, N_HEADS_WIN, N_HEADS_MEM)
    gq = jnp.concatenate([jnp.tile(q_norm_g[:, i], (1, n)) for i, n in enumerate(heads)], axis=-1) * q_scale
    gk = jnp.concatenate([jnp.tile(k_norm_g[:, 0], (1, N_HEADS_NA)), jnp.tile(k_norm_g[:, 1], (1, N_KV_WIN))], axis=-1)
    gkm = jnp.tile(k_norm_g[:, 2], (1, N_HEADS_MEM))
    sink2 = sink.astype(_F32) * LOG2E
    rpb2 = rpb.astype(_F32) * LOG2E

    q_bound = HEAD_DIM ** 0.5 * jnp.max(jnp.abs(q_norm_g), axis=-1) * q_scale
    k_bound = HEAD_DIM ** 0.5 * jnp.max(jnp.abs(k_norm_g), axis=-1)
    qk = q_bound * k_bound
    rpb_max, rpb_min = jnp.max(rpb2, axis=(2, 3)), jnp.min(rpb2, axis=(2, 3))
    shift_na = qk[:, 0:1] + rpb_max
    shift_win = jnp.maximum(qk[:, 1:2], sink2)
    shift_mem = jnp.broadcast_to(qk[:, 2:3], (depth, N_HEADS_MEM))
    slack = jnp.max(2.0 * qk + jnp.stack([jnp.max(rpb_max - rpb_min, axis=1), jnp.zeros((depth,), _F32),
                                          jnp.zeros((depth,), _F32)], axis=1), axis=1)
    bound_ok = slack <= MAX_SHIFT_SLACK
    use = bound_ok[:, None].astype(_F32)
    shift_na, shift_win, shift_mem = shift_na * use, shift_win * use, shift_mem * use
    sink_logit = jnp.repeat(sink2 - shift_win, WINDOW, axis=-1)
    return dict(
        ng=norm_g.reshape(depth, 1, D_MODEL), wt=wt,
        gq=gq.reshape(depth, D_MIX, 1), gk=gk.reshape(depth, D_K, 1), gkm=gkm.reshape(depth, D_MEM, 1),
        bna=_na_bias(rpb2, shift_na), bwin=_win_bias(shift_win),
        sink_rows=jnp.stack([sink_logit, jnp.exp2(sink_logit)], axis=1),
        bmem=jnp.broadcast_to(-shift_mem[:, :, None, None], (depth, N_HEADS_MEM, 1, TILE)),
        mg=mem_norm_g.reshape(depth, 1, D_MODEL),
        wmt=jnp.swapaxes(w_mem_kv, 1, 2).astype(_BF16),
        wo=w_out.astype(_BF16),
        bound_ok=bound_ok,
    )


def _forward(xs, mems, norm_g, w_in, q_norm_g, k_norm_g, rpb, sink, mem_norm_g, w_mem_kv, w_out):
    depth = w_in.shape[0]
    w = _prep_weights(norm_g, w_in, q_norm_g, k_norm_g, rpb, sink, mem_norm_g, w_mem_kv, w_out)
    mem_kv = [_memkv_call(mem, w["mg"], w["wmt"], w["gkm"]) for mem in mems]
    xs = list(xs)
    mixes = [None] * len(xs)
    tables = (w["bna"], w["bwin"], w["sink_rows"], w["bmem"])
    for l in range(depth):
        for gi in range(len(xs)):
            outs = _proj_call(l, xs[gi], w, mixes[gi])
            if mixes[gi] is not None:
                xs[gi], outs = outs[0], outs[1:]
            q, k, v, g = outs
            mixes[gi] = lax.cond(w["bound_ok"][l],
                                 functools.partial(_attn_call, l, False),
                                 functools.partial(_attn_call, l, True),
                                 q, g, k, v, *mem_kv[gi], *tables)
    return tuple(_proj_call(None, x, w, mix)[0] for x, mix in zip(xs, mixes))


def kernel(x_prompt, x_sample, mem_prompt, mem_sample, norm_g, w_in, q_norm_g, k_norm_g, rpb, sink, mem_norm_g, w_mem_kv, w_out):
    return _forward((x_prompt, x_sample), (mem_prompt, mem_sample), norm_g, w_in, q_norm_g, k_norm_g, rpb, sink,
                    mem_norm_g, w_mem_kv, w_out)
```
